```python
import jax, jax.numpy as jnp
from jax import lax
import numpy as np

D_MODEL = 1024
BATCH = 8
SEQ = 2048
DEPTH = 4

CHUNK = 64

D_MIX = D_MODEL
D_CONV = D_MIX // 2
D_POOL = D_MIX - D_CONV
CONV_HEADS = 8
CONV_WIDTH = 31
POOL_WINDOWS = (2, 4, 8, 16)
N_POOL_GROUPS = len(POOL_WINDOWS)
POOL_GROUP = D_POOL // N_POOL_GROUPS
D_IN = 2 * D_CONV + D_POOL

PEER_HEADS = 8
N_KEYS = 128
N_EXPERTS = N_KEYS * N_KEYS
D_KEY = 256
D_HALF = D_KEY // 2
PEER_TOPK = 16
PEER_BLOCK = 128

EPS = 1e-6

kernel_name = "hybrid_conv_pool_peer_encoder"


def rmsnorm(x, g):
    xf = x.astype(jnp.float32)
    y = xf * lax.rsqrt(jnp.mean(xf * xf, axis=-1, keepdims=True) + EPS)
    return (y * g.astype(jnp.float32)).astype(x.dtype)


def modulate(h, shift, scale):
    return h * (1 + scale[:, None, :]) + shift[:, None, :]


def conv_mixer(z_val, z_gate, w_dw, b_dw, ln_g, ln_b):
    u = z_val * jax.nn.sigmoid(z_gate)
    y = lax.conv_general_dilated(
        u, w_dw[:, None, :], window_strides=(1,),
        padding=[(CONV_WIDTH - 1, 0)],
        dimension_numbers=("NWC", "WIO", "NWC"),
        feature_group_count=D_CONV) + b_dw
    yf = y.astype(jnp.float32)
    mu = jnp.mean(yf, axis=-1, keepdims=True)
    var = jnp.mean(jnp.square(yf - mu), axis=-1, keepdims=True)
    yn = (yf - mu) * lax.rsqrt(var + EPS) * ln_g.astype(jnp.float32) + ln_b.astype(jnp.float32)
    return jax.nn.silu(yn).astype(z_val.dtype)


def pool_mixer(z, w_grp, b_grp, scale):
    B, S, _ = z.shape
    zf = z.astype(jnp.float32)
    cs = jnp.cumsum(zf, axis=1)
    t1 = jnp.arange(1, S + 1, dtype=jnp.float32)[None, :, None]
    outs = []
    for g, w in enumerate(POOL_WINDOWS):
        sl = slice(g * POOL_GROUP, (g + 1) * POOL_GROUP)
        csg = cs[:, :, sl]
        lagged = jnp.pad(csg, ((0, 0), (w, 0), (0, 0)))[:, :S]
        cnt = jnp.minimum(t1, float(w))
        outs.append((csg - lagged) / cnt - zf[:, :, sl])
    p = jnp.concatenate(outs, axis=-1).astype(z.dtype).reshape(B, S, N_POOL_GROUPS, POOL_GROUP)
    y = jnp.einsum("bsgc,gcd->bsgd", p, w_grp) + b_grp
    return y.reshape(B, S, D_POOL) * scale


def peer(h, w_q, sub_keys, u_tab, v_tab):
    B, S, D = h.shape
    T = B * S
    hf = h.reshape(T, D)
    q = (hf @ w_q).reshape(T, PEER_HEADS, 2, D_HALF)
    s = jnp.einsum("thpd,hpkd->thpk", q, sub_keys).astype(jnp.float32)
    top_v, top_i = lax.top_k(s, PEER_TOPK)
    comb = (top_v[:, :, 0, :, None] + top_v[:, :, 1, None, :]).reshape(
        T, PEER_HEADS, PEER_TOPK * PEER_TOPK)
    best_v, best_j = lax.top_k(comb, PEER_TOPK)
    i1 = jnp.take_along_axis(top_i[:, :, 0], best_j // PEER_TOPK, axis=-1)
    i2 = jnp.take_along_axis(top_i[:, :, 1], best_j % PEER_TOPK, axis=-1)
    idx = (i1 * N_KEYS + i2).astype(jnp.int32)
    gate = jax.nn.softmax(best_v, axis=-1).astype(h.dtype)
    M = PEER_HEADS * PEER_TOPK
    nb = T // PEER_BLOCK
    xs = hf.reshape(nb, PEER_BLOCK, D)
    ids = idx.reshape(nb, PEER_BLOCK, M)
    gs = gate.reshape(nb, PEER_BLOCK, M)

    def block(args):
        xb, ib, gb = args
        a = jnp.einsum("tmd,td->tm", u_tab[ib], xb)
        wgt = gb * jax.nn.gelu(a, approximate=False)
        return jnp.einsum("tm,tmd->td", wgt, v_tab[ib])

    y = lax.map(block, (xs, ids, gs))
    return y.reshape(B, S, D)


def setup_inputs(seed: int = 0) -> dict:
    key = jax.random.key(seed)
    ks = jax.random.split(key, 24)
    f32 = jnp.float32
    D = D_MODEL

    def nrm(k, shape, scale):
        return jax.random.normal(k, shape, f32) * scale

    return {
        "x": nrm(ks[0], (BATCH, SEQ, D), 1.0),
        "c": nrm(ks[1], (BATCH, D), 1.0),
        "w_ada": nrm(ks[2], (DEPTH, D, 6 * D), 0.5 * D ** -0.5),
        "b_ada": nrm(ks[3], (DEPTH, 6 * D), 0.02),
        "norm1_g": 1.0 + nrm(ks[4], (DEPTH, D), 0.05),
        "norm2_g": 1.0 + nrm(ks[5], (DEPTH, D), 0.05),
        "w_in": nrm(ks[6], (DEPTH, D, D_IN), D ** -0.5),
        "w_out": nrm(ks[7], (DEPTH, D_MIX, D), D_MIX ** -0.5),
        "conv_w": nrm(ks[8], (DEPTH, CONV_WIDTH, D_CONV), CONV_WIDTH ** -0.5),
        "conv_b": nrm(ks[9], (DEPTH, D_CONV), 0.02),
        "conv_ln_g": 1.0 + nrm(ks[10], (DEPTH, D_CONV), 0.05),
        "conv_ln_b": nrm(ks[11], (DEPTH, D_CONV), 0.02),
        "pool_w": nrm(ks[12], (DEPTH, N_POOL_GROUPS, POOL_GROUP, POOL_GROUP), POOL_GROUP ** -0.5),
        "pool_b": nrm(ks[13], (DEPTH, N_POOL_GROUPS, POOL_GROUP), 0.02),
        "pool_scale": 1.0 + nrm(ks[14], (DEPTH, D_POOL), 0.1),
        "peer_wq": nrm(ks[15], (DEPTH, D, PEER_HEADS * D_KEY), D ** -0.5),
        "peer_keys": nrm(ks[16], (DEPTH, PEER_HEADS, 2, N_KEYS, D_HALF), D_HALF ** -0.5),
        "peer_u": nrm(ks[17], (DEPTH, N_EXPERTS, D), D ** -0.5),
        "peer_v": nrm(ks[18], (DEPTH, N_EXPERTS, D), (PEER_HEADS * PEER_TOPK) ** -0.5),
        "final_g": 1.0 + nrm(ks[19], (D,), 0.05),
    }


def reference(x, c, w_ada, b_ada, norm1_g, norm2_g, w_in, w_out, conv_w, conv_b,
              conv_ln_g, conv_ln_b, pool_w, pool_b, pool_scale, peer_wq, peer_keys,
              peer_u, peer_v, final_g):
    cond = jax.nn.silu(c)
    for l in range(DEPTH):
        mod = cond @ w_ada[l] + b_ada[l]
        sh1, sc1, g1, sh2, sc2, g2 = jnp.split(mod, 6, axis=-1)

        h = modulate(rmsnorm(x, norm1_g[l]), sh1, sc1)
        z = h @ w_in[l]
        ya = conv_mixer(z[..., :D_CONV], z[..., D_CONV:2 * D_CONV],
                        conv_w[l], conv_b[l], conv_ln_g[l], conv_ln_b[l])
        yb = pool_mixer(z[..., 2 * D_CONV:], pool_w[l], pool_b[l], pool_scale[l])
        mix = jnp.concatenate([ya, yb], axis=-1) @ w_out[l]
        x = x + g1[:, None, :] * mix

        h = modulate(rmsnorm(x, norm2_g[l]), sh2, sc2)
        x = x + g2[:, None, :] * peer(h, peer_wq[l], peer_keys[l], peer_u[l], peer_v[l])
    return rmsnorm(x, final_g)
```

```python
import functools

import jax
import jax.numpy as jnp
from jax import lax
from jax.experimental import pallas as pl
from jax.experimental.pallas import tpu as pltpu

EPS = 1e-6
POOL_WINDOWS = (2, 4, 8, 16)
PEER_TOPK = 16
LANES = 128
HALO = 32
CONV_ROWS = 32
P_PITCH = 136
VMEM_LIMIT = 56 * 1024 * 1024

F32 = jnp.float32
BF16 = jnp.bfloat16
NEG_INF = float("-inf")


def _rms_modulate(x, gain, shift, scale):
    ms = jnp.mean(x * x, axis=-1, keepdims=True)
    return (x * lax.rsqrt(ms + EPS) * gain) * (1.0 + scale) + shift


def _adaln_kernel(c_ref, w_ref, b_ref, o_ref):
    c = c_ref[...]
    cond = c * jax.nn.sigmoid(c)
    o_ref[...] = jnp.dot(cond, w_ref[...], preferred_element_type=F32,
                         precision=lax.Precision.HIGHEST) + b_ref[...]


def _adaln(c, w_ada, b_ada):
    depth, d, n = w_ada.shape
    b = c.shape[0]
    tn = 1536
    return pl.pallas_call(
        _adaln_kernel,
        grid=(depth, n // tn),
        in_specs=[
            pl.BlockSpec((b, d), lambda l, j: (0, 0)),
            pl.BlockSpec((None, d, tn), lambda l, j: (l, 0, j)),
            pl.BlockSpec((None, 1, tn), lambda l, j: (l, 0, j)),
        ],
        out_specs=pl.BlockSpec((None, b, tn), lambda l, j: (l, 0, j)),
        out_shape=jax.ShapeDtypeStruct((depth, b, n), F32),
        compiler_params=pltpu.CompilerParams(vmem_limit_bytes=VMEM_LIMIT),
        name="adaln",
    )(c, w_ada, b_ada.reshape(depth, 1, n))


def _mixer_kernel(x_ref, mod_ref, n1g_ref, win_ref, convw_ref, convb_ref, lng_ref, lnb_ref,
                  poolw_ref, poolb_ref, pscale_ref, wout_ref, o_ref,
                  ubuf, zbuf, ya_scr, yb_scr, *, ts, d_conv, conv_width):
    s = pl.program_id(1)
    pool_group = poolw_ref.shape[-1]

    @pl.when(s == 0)
    def _():
        ubuf[0:HALO, :] = jnp.zeros((HALO, ubuf.shape[1]), F32)
        zbuf[0:HALO, :] = jnp.zeros((HALO, zbuf.shape[1]), F32)

    x = x_ref[...]
    h = _rms_modulate(x, n1g_ref[...], mod_ref[0:1, :], mod_ref[1:2, :])
    z = jnp.dot(h.astype(BF16), win_ref[...], preferred_element_type=F32)
    ubuf[HALO:HALO + ts, :] = z[:, :d_conv] * jax.nn.sigmoid(z[:, d_conv:2 * d_conv])
    zbuf[HALO:HALO + ts, :] = z[:, 2 * d_conv:]

    convb = convb_ref[...]
    lng = lng_ref[...]
    lnb = lnb_ref[...]
    for r0 in range(0, ts, CONV_ROWS):
        acc = jnp.broadcast_to(convb, (CONV_ROWS, d_conv))
        for k in range(conv_width):
            start = HALO - (conv_width - 1) + k + r0
            acc = acc + convw_ref[k:k + 1, :] * ubuf[start:start + CONV_ROWS, :]
        mu = jnp.mean(acc, axis=-1, keepdims=True)
        cen = acc - mu
        var = jnp.mean(cen * cen, axis=-1, keepdims=True)
        yn = cen * lax.rsqrt(var + EPS) * lng + lnb
        ya_scr[r0:r0 + CONV_ROWS, :] = (yn * jax.nn.sigmoid(yn)).astype(BF16)

    frame = (s * ts + lax.broadcasted_iota(jnp.int32, (ts, 1), 0) + 1).astype(F32)
    for g, w in enumerate(POOL_WINDOWS):
        lo = g * pool_group
        zg = zbuf[HALO:HALO + ts, lo:lo + pool_group]
        tot = zg
        for j in range(1, w):
            tot = tot + zbuf[HALO - j:HALO - j + ts, lo:lo + pool_group]
        p = tot / jnp.minimum(frame, float(w)) - zg
        yg = jnp.dot(p.astype(BF16), poolw_ref[g], preferred_element_type=F32) + poolb_ref[g:g + 1, :]
        yb_scr[:, lo:lo + pool_group] = (yg * pscale_ref[:, lo:lo + pool_group]).astype(BF16)

    mix = jnp.dot(ya_scr[...], wout_ref[0:d_conv, :], preferred_element_type=F32)
    mix = mix + jnp.dot(yb_scr[...], wout_ref[d_conv:, :], preferred_element_type=F32)
    o_ref[...] = x + mod_ref[2:3, :] * mix

    ubuf[0:HALO, :] = ubuf[ts:ts + HALO, :]
    zbuf[0:HALO, :] = zbuf[ts:ts + HALO, :]


def _mixer(x, mod, n1g, w_in, conv_w, conv_b, ln_g, ln_b, pool_w, pool_b, pool_scale, w_out, l):
    b, s, d = x.shape
    d_in = w_in.shape[-1]
    conv_width, d_conv = conv_w.shape[1:]
    n_groups, pool_group = pool_w.shape[1:3]
    d_pool = n_groups * pool_group
    ts = min(256, s)
    assert s % ts == 0 and ts % CONV_ROWS == 0 and ts >= HALO
    assert HALO >= conv_width - 1 and HALO >= max(POOL_WINDOWS) - 1
    assert d_in == 2 * d_conv + d_pool and n_groups == len(POOL_WINDOWS)
    lay = lambda bi, si: (l, 0, 0)
    kern = functools.partial(_mixer_kernel, ts=ts, d_conv=d_conv, conv_width=conv_width)
    return pl.pallas_call(
        kern,
        grid=(b, s // ts),
        in_specs=[
            pl.BlockSpec((None, ts, d), lambda bi, si: (bi, si, 0)),
            pl.BlockSpec((None, None, 6, d), lambda bi, si: (l, bi, 0, 0)),
            pl.BlockSpec((None, 1, d), lay),
            pl.BlockSpec((None, d, d_in), lay),
            pl.BlockSpec((None, conv_width, d_conv), lay),
            pl.BlockSpec((None, 1, d_conv), lay),
            pl.BlockSpec((None, 1, d_conv), lay),
            pl.BlockSpec((None, 1, d_conv), lay),
            pl.BlockSpec((None, n_groups, pool_group, pool_group), lambda bi, si: (l, 0, 0, 0)),
            pl.BlockSpec((None, n_groups, pool_group), lay),
            pl.BlockSpec((None, 1, d_pool), lay),
            pl.BlockSpec((None, d_conv + d_pool, d), lay),
        ],
        out_specs=pl.BlockSpec((None, ts, d), lambda bi, si: (bi, si, 0)),
        out_shape=jax.ShapeDtypeStruct((b, s, d), F32),
        scratch_shapes=[
            pltpu.VMEM((HALO + ts, d_conv), F32),
            pltpu.VMEM((HALO + ts, d_pool), F32),
            pltpu.VMEM((ts, d_conv), BF16),
            pltpu.VMEM((ts, d_pool), BF16),
        ],
        compiler_params=pltpu.CompilerParams(
            dimension_semantics=("arbitrary", "arbitrary"), vmem_limit_bytes=VMEM_LIMIT),
        name="mixer",
    )(x, mod, n1g, w_in, conv_w, conv_b, ln_g, ln_b, pool_w, pool_b, pool_scale, w_out)


def _route_kernel(x_ref, mod_ref, n2g_ref, wq_ref, keys_ref, h_ref, i1_ref, i2_ref, gate_ref,
                  q_scr, tv_scr, ti_scr, e_scr, g_scr, *, tt, heads, n_keys):
    k = PEER_TOPK
    x = x_ref[...]
    h = _rms_modulate(x, n2g_ref[...], mod_ref[3:4, :], mod_ref[4:5, :]).astype(BF16)
    h_ref[...] = h
    q = jnp.dot(h, wq_ref[...], preferred_element_type=F32).astype(BF16)
    half = keys_ref.shape[-1]
    for hp in range(2 * heads):
        q_scr[hp] = q[:, hp * half:(hp + 1) * half]

    key_id = lax.broadcasted_iota(jnp.int32, (n_keys, tt), 0).astype(F32)

    def stage1(hp, carry):
        sc = lax.dot_general(keys_ref[hp], q_scr[hp], (((1,), (1,)), ((), ())),
                             preferred_element_type=F32)
        for j in range(k):
            m = jnp.max(sc, axis=0, keepdims=True)
            idx = jnp.min(jnp.where(sc == m, key_id, float(n_keys)), axis=0, keepdims=True)
            sc = jnp.where(key_id == idx, NEG_INF, sc)
            tv_scr[hp, j:j + 1, :] = m
            ti_scr[hp, j:j + 1, :] = idx
        return carry

    lax.fori_loop(0, 2 * heads, stage1, 0)

    b_full = lax.broadcasted_iota(jnp.int32, (k, tt), 0).astype(F32)
    b_half = lax.broadcasted_iota(jnp.int32, (k // 2, tt), 0).astype(F32)

    def stage2(hd, carry):
        tv1 = tv_scr[2 * hd]
        tv2 = tv_scr[2 * hd + 1]
        ti1 = ti_scr[2 * hd]
        ti2 = ti_scr[2 * hd + 1]
        vals = [tv1[0:1, :] + tv2]
        flat = [b_full]
        pay = [ti1[0:1, :] * float(n_keys) + ti2]
        for a in range(1, k // 2):
            v = tv1[a:a + 1, :] + tv2[0:k // 2, :]
            lim = k // (a + 1)
            if lim < k // 2:
                v = jnp.where(b_half < float(lim), v, NEG_INF)
            vals.append(v)
            flat.append(b_half + float(a * k))
            pay.append(ti1[a:a + 1, :] * float(n_keys) + ti2[0:k // 2, :])
        vals.append(tv1[k // 2:, :] + tv2[0:1, :])
        flat.append((b_half + float(k // 2)) * float(k))
        pay.append(ti1[k // 2:, :] * float(n_keys) + ti2[0:1, :])
        cand = jnp.concatenate(vals, axis=0)
        flat = jnp.concatenate(flat, axis=0)
        pay = jnp.concatenate(pay, axis=0)
        best = []
        for j in range(k):
            m = jnp.max(cand, axis=0, keepdims=True)
            sel = jnp.min(jnp.where(cand == m, flat, float(k * k)), axis=0, keepdims=True)
            hit = flat == sel
            e_scr[pl.ds(hd * k + j, 1), :] = jnp.max(jnp.where(hit, pay, -1.0), axis=0, keepdims=True)
            cand = jnp.where(hit, NEG_INF, cand)
            best.append(m)
        ex = [jnp.exp(m - best[0]) for m in best]
        den = ex[0]
        for j in range(1, k):
            den = den + ex[j]
        for j in range(k):
            g_scr[pl.ds(hd * k + j, 1), :] = ex[j] / den
        return carry

    lax.fori_loop(0, heads, stage2, 0)

    e = e_scr[...].T.astype(jnp.int32)
    key_bits = n_keys.bit_length() - 1
    i1_ref[...] = lax.shift_right_logical(e, key_bits)
    i2_ref[...] = lax.bitwise_and(e, n_keys - 1)
    gate_ref[...] = g_scr[...].T


def _route(x2, mod, n2g, wq, keys, l, seq):
    t, d = x2.shape
    heads, _, n_keys, half = keys.shape[1:]
    m = heads * PEER_TOPK
    tt = 256
    assert seq % tt == 0 and m == LANES and n_keys == LANES and n_keys & (n_keys - 1) == 0
    keys2 = keys.reshape(keys.shape[0], 2 * heads, n_keys, half)
    kern = functools.partial(_route_kernel, tt=tt, heads=heads, n_keys=n_keys)
    tok = lambda i: (i, 0)
    return pl.pallas_call(
        kern,
        grid=(t // tt,),
        in_specs=[
            pl.BlockSpec((tt, d), tok),
            pl.BlockSpec((None, None, 6, d), lambda i: (l, (i * tt) // seq, 0, 0)),
            pl.BlockSpec((None, 1, d), lambda i: (l, 0, 0)),
            pl.BlockSpec((None, d, 2 * heads * half), lambda i: (l, 0, 0)),
            pl.BlockSpec((None, 2 * heads, n_keys, half), lambda i: (l, 0, 0, 0)),
        ],
        out_specs=[pl.BlockSpec((tt, d), tok), pl.BlockSpec((tt, m), tok),
                   pl.BlockSpec((tt, m), tok), pl.BlockSpec((tt, m), tok)],
        out_shape=[jax.ShapeDtypeStruct((t, d), BF16), jax.ShapeDtypeStruct((t, m), jnp.int32),
                   jax.ShapeDtypeStruct((t, m), jnp.int32), jax.ShapeDtypeStruct((t, m), F32)],
        scratch_shapes=[
            pltpu.VMEM((2 * heads, tt, half), BF16),
            pltpu.VMEM((2 * heads, PEER_TOPK, tt), F32),
            pltpu.VMEM((2 * heads, PEER_TOPK, tt), F32),
            pltpu.VMEM((m, tt), F32),
            pltpu.VMEM((m, tt), F32),
        ],
        compiler_params=pltpu.CompilerParams(
            dimension_semantics=("arbitrary",), vmem_limit_bytes=VMEM_LIMIT),
        name="route",
    )(x2, mod, n2g, wq, keys2)


def _expert_in_kernel(h_ref, ut_ref, i1_ref, i2_ref, a_ref, *, te):
    j = pl.program_id(1)

    @pl.when(j == 0)
    def _():
        a_ref[...] = jnp.zeros(a_ref.shape, F32)

    a_all = jnp.dot(h_ref[...], ut_ref[...], preferred_element_type=F32)
    i1 = i1_ref[...]
    i2 = i2_ref[...]
    acc = a_ref[...]
    for c in range(te // LANES):
        picked = jnp.take_along_axis(a_all[:, c * LANES:(c + 1) * LANES], i2, axis=1)
        acc = jnp.where(i1 == j * (te // LANES) + c, picked, acc)
    a_ref[...] = acc


def _expert_in(h2, ut, i1, i2, l):
    t, d = h2.shape
    n_exp = ut.shape[-1]
    m = i1.shape[1]
    tm, te = min(512, t), 2048
    assert t % tm == 0 and n_exp % te == 0
    tok = lambda i, j: (i, 0)
    return pl.pallas_call(
        functools.partial(_expert_in_kernel, te=te),
        grid=(t // tm, n_exp // te),
        in_specs=[
            pl.BlockSpec((tm, d), tok),
            pl.BlockSpec((None, d, te), lambda i, j: (l, 0, j)),
            pl.BlockSpec((tm, m), tok),
            pl.BlockSpec((tm, m), tok),
        ],
        out_specs=pl.BlockSpec((tm, m), tok),
        out_shape=jax.ShapeDtypeStruct((t, m), F32),
        compiler_params=pltpu.CompilerParams(
            dimension_semantics=("arbitrary", "arbitrary"), vmem_limit_bytes=VMEM_LIMIT),
        name="expert_in",
    )(h2, ut, i1, i2)


def _expert_out_kernel(a_ref, gate_ref, i1_ref, i2_ref, v_ref, x_ref, mod_ref, fg_ref, o_ref,
                       w_scr, pbuf, lhs_scr, acc_scr, *, tm, te, n_keys, final_norm):
    j = pl.program_id(1)
    nk = te // LANES

    @pl.when(j == 0)
    def _():
        a = a_ref[...]
        w_scr[...] = gate_ref[...] * (0.5 * a * (1.0 + lax.erf(a * (2.0 ** -0.5))))
        acc_scr[...] = jnp.zeros(acc_scr.shape, F32)
        key_id = lax.broadcasted_iota(jnp.int32, (n_keys, LANES), 0)

        def place(t, carry):
            w_row = w_scr[pl.ds(t, 1), :]
            left = jnp.where(key_id == i1_ref[pl.ds(t, 1), :], w_row, 0.0).astype(BF16)
            right = jnp.where(key_id == i2_ref[pl.ds(t, 1), :], 1.0, 0.0).astype(BF16)
            pbuf[pl.ds(pl.multiple_of(t * P_PITCH, 8), n_keys), :] = lax.dot_general(
                left, right, (((1,), (1,)), ((), ())), preferred_element_type=F32)
            return carry

        lax.fori_loop(0, tm, place, 0)

    for c in range(nk):
        lhs_scr[:, c * LANES:(c + 1) * LANES] = pbuf[pl.ds(j * nk + c, tm, stride=P_PITCH), :].astype(BF16)
    acc_scr[...] += jnp.dot(lhs_scr[...], v_ref[...], preferred_element_type=F32)

    @pl.when(j == pl.num_programs(1) - 1)
    def _():
        y = x_ref[...] + mod_ref[5:6, :] * acc_scr[...]
        if final_norm:
            ms = jnp.mean(y * y, axis=-1, keepdims=True)
            y = y * lax.rsqrt(ms + EPS) * fg_ref[...]
        o_ref[...] = y


def _expert_out(a_sel, gate, i1, i2, vb, x2, mod, final_g, l, seq, final_norm):
    t, d = x2.shape
    n_exp = vb.shape[1]
    m = i1.shape[1]
    n_keys = LANES
    tm, te = 256, 2048
    assert t % tm == 0 and seq % tm == 0 and n_exp % te == 0 and n_exp == n_keys * n_keys
    tok = lambda i, j: (i, 0)
    kern = functools.partial(_expert_out_kernel, tm=tm, te=te, n_keys=n_keys, final_norm=final_norm)
    return pl.pallas_call(
        kern,
        grid=(t // tm, n_exp // te),
        in_specs=[
            pl.BlockSpec((tm, m), tok),
            pl.BlockSpec((tm, m), tok),
            pl.BlockSpec((tm, m), tok),
            pl.BlockSpec((tm, m), tok),
            pl.BlockSpec((None, te, d), lambda i, j: (l, j, 0)),
            pl.BlockSpec((tm, d), tok),
            pl.BlockSpec((None, None, 6, d), lambda i, j: (l, (i * tm) // seq, 0, 0)),
            pl.BlockSpec((1, d), lambda i, j: (0, 0)),
        ],
        out_specs=pl.BlockSpec((tm, d), tok),
        out_shape=jax.ShapeDtypeStruct((t, d), F32),
        scratch_shapes=[
            pltpu.VMEM((tm, m), F32),
            pltpu.VMEM((tm * P_PITCH, LANES), F32),
            pltpu.VMEM((tm, te), BF16),
            pltpu.VMEM((tm, d), F32),
        ],
        compiler_params=pltpu.CompilerParams(
            dimension_semantics=("arbitrary", "arbitrary"), vmem_limit_bytes=VMEM_LIMIT),
        name="expert_out",
    )(a_sel, gate, i1, i2, vb, x2, mod, final_g)


def kernel(x, c, w_ada, b_ada, norm1_g, norm2_g, w_in, w_out, conv_w, conv_b, conv_ln_g, conv_ln_b,
           pool_w, pool_b, pool_scale, peer_wq, peer_keys, peer_u, peer_v, final_g):
    depth = w_ada.shape[0]
    b, s, d = x.shape
    row = lambda p: p.reshape(depth, 1, p.shape[-1])

    mod = _adaln(c, w_ada, b_ada).reshape(depth, b, 6, d)
    w_in_b = w_in.astype(BF16)
    w_out_b = w_out.astype(BF16)
    pool_w_b = pool_w.astype(BF16)
    wq_b = peer_wq.astype(BF16)
    keys_b = peer_keys.astype(BF16)
    ut_b = jnp.swapaxes(peer_u, 1, 2).astype(BF16)
    v_b = peer_v.astype(BF16)
    n1g, n2g = row(norm1_g), row(norm2_g)
    cb, lg, lb, ps = row(conv_b), row(conv_ln_g), row(conv_ln_b), row(pool_scale)
    fg = final_g.reshape(1, d)

    for l in range(depth):
        x = _mixer(x, mod, n1g, w_in_b, conv_w, cb, lg, lb, pool_w_b, pool_b, ps, w_out_b, l)
        x2 = x.reshape(b * s, d)
        h2, i1, i2, gate = _route(x2, mod, n2g, wq_b, keys_b, l, s)
        a_sel = _expert_in(h2, ut_b, i1, i2, l)
        x2 = _expert_out(a_sel, gate, i1, i2, v_b, x2, mod, fg, l, s, l == depth - 1)
        x = x2.reshape(b, s, d)
    return x
```

```python
import functools

import jax
import jax.numpy as jnp
from jax import lax
from jax.experimental import pallas as pl
from jax.experimental.pallas import tpu as pltpu

EPS = 1e-6
POOL_WINDOWS = (2, 4, 8, 16)
PEER_TOPK = 16
LANES = 128
HALO = 32
CONV_ROWS = 32
P_PITCH = 136
VMEM_LIMIT = 56 * 1024 * 1024

F32 = jnp.float32
BF16 = jnp.bfloat16
NEG_INF = float("-inf")


def _rms_modulate(x, gain, shift, scale):
    ms = jnp.mean(x * x, axis=-1, keepdims=True)
    return (x * lax.rsqrt(ms + EPS) * gain) * (1.0 + scale) + shift


def _adaln_kernel(c_ref, w_ref, b_ref, o_ref):
    c = c_ref[...]
    cond = c * jax.nn.sigmoid(c)
    o_ref[...] = jnp.dot(cond, w_ref[...], preferred_element_type=F32,
                         precision=lax.Precision.HIGHEST) + b_ref[...]


def _adaln(c, w_ada, b_ada):
    depth, d, n = w_ada.shape
    b = c.shape[0]
    tn = 1536
    return pl.pallas_call(
        _adaln_kernel,
        grid=(depth, n // tn),
        in_specs=[
            pl.BlockSpec((b, d), lambda l, j: (0, 0)),
            pl.BlockSpec((None, d, tn), lambda l, j: (l, 0, j)),
            pl.BlockSpec((None, 1, tn), lambda l, j: (l, 0, j)),
        ],
        out_specs=pl.BlockSpec((None, b, tn), lambda l, j: (l, 0, j)),
        out_shape=jax.ShapeDtypeStruct((depth, b, n), F32),
        compiler_params=pltpu.CompilerParams(vmem_limit_bytes=VMEM_LIMIT),
        name="adaln",
    )(c, w_ada, b_ada.reshape(depth, 1, n))


def _mixer_kernel(x_ref, mod_ref, n1g_ref, win_ref, convw_ref, convb_ref, lng_ref, lnb_ref,
                  poolw_ref, poolb_ref, pscale_ref, wout_ref, o_ref,
                  ubuf, zbuf, ya_scr, yb_scr, *, ts, d_conv, conv_width):
    s = pl.program_id(1)
    pool_group = poolw_ref.shape[-1]

    @pl.when(s == 0)
    def _():
        ubuf[0:HALO, :] = jnp.zeros((HALO, ubuf.shape[1]), F32)
        zbuf[0:HALO, :] = jnp.zeros((HALO, zbuf.shape[1]), F32)

    x = x_ref[...]
    h = _rms_modulate(x, n1g_ref[...], mod_ref[0:1, :], mod_ref[1:2, :])
    z = jnp.dot(h.astype(BF16), win_ref[...], preferred_element_type=F32)
    ubuf[HALO:HALO + ts, :] = z[:, :d_conv] * jax.nn.sigmoid(z[:, d_conv:2 * d_conv])
    zbuf[HALO:HALO + ts, :] = z[:, 2 * d_conv:]

    convb = convb_ref[...]
    lng = lng_ref[...]
    lnb = lnb_ref[...]
    for r0 in range(0, ts, CONV_ROWS):
        acc = jnp.broadcast_to(convb, (CONV_ROWS, d_conv))
        for k in range(conv_width):
            start = HALO - (conv_width - 1) + k + r0
            acc = acc + convw_ref[k:k + 1, :] * ubuf[start:start + CONV_ROWS, :]
        mu = jnp.mean(acc, axis=-1, keepdims=True)
        cen = acc - mu
        var = jnp.mean(cen * cen, axis=-1, keepdims=True)
        yn = cen * lax.rsqrt(var + EPS) * lng + lnb
        ya_scr[r0:r0 + CONV_ROWS, :] = (yn * jax.nn.sigmoid(yn)).astype(BF16)

    frame = (s * ts + lax.broadcasted_iota(jnp.int32, (ts, 1), 0) + 1).astype(F32)
    for g, w in enumerate(POOL_WINDOWS):
        lo = g * pool_group
        zg = zbuf[HALO:HALO + ts, lo:lo + pool_group]
        tot = zg
        for j in range(1, w):
            tot = tot + zbuf[HALO - j:HALO - j + ts, lo:lo + pool_group]
        p = tot / jnp.minimum(frame, float(w)) - zg
        yg = jnp.dot(p.astype(BF16), poolw_ref[g], preferred_element_type=F32) + poolb_ref[g:g + 1, :]
        yb_scr[:, lo:lo + pool_group] = (yg * pscale_ref[:, lo:lo + pool_group]).astype(BF16)

    mix = jnp.dot(ya_scr[...], wout_ref[0:d_conv, :], preferred_element_type=F32)
    mix = mix + jnp.dot(yb_scr[...], wout_ref[d_conv:, :], preferred_element_type=F32)
    o_ref[...] = x + mod_ref[2:3, :] * mix

    ubuf[0:HALO, :] = ubuf[ts:ts + HALO, :]
    zbuf[0:HALO, :] = zbuf[ts:ts + HALO, :]


def _mixer(x, mod, n1g, w_in, conv_w, conv_b, ln_g, ln_b, pool_w, pool_b, pool_scale, w_out, l):
    b, s, d = x.shape
    d_in = w_in.shape[-1]
    conv_width, d_conv = conv_w.shape[1:]
    n_groups, pool_group = pool_w.shape[1:3]
    d_pool = n_groups * pool_group
    ts = min(256, s)
    assert s % ts == 0 and ts % CONV_ROWS == 0 and ts >= HALO
    assert HALO >= conv_width - 1 and HALO >= max(POOL_WINDOWS) - 1
    assert d_in == 2 * d_conv + d_pool and n_groups == len(POOL_WINDOWS)
    lay = lambda bi, si: (l, 0, 0)
    kern = functools.partial(_mixer_kernel, ts=ts, d_conv=d_conv, conv_width=conv_width)
    return pl.pallas_call(
        kern,
        grid=(b, s // ts),
        in_specs=[
            pl.BlockSpec((None, ts, d), lambda bi, si: (bi, si, 0)),
            pl.BlockSpec((None, None, 6, d), lambda bi, si: (l, bi, 0, 0)),
            pl.BlockSpec((None, 1, d), lay),
            pl.BlockSpec((None, d, d_in), lay),
            pl.BlockSpec((None, conv_width, d_conv), lay),
            pl.BlockSpec((None, 1, d_conv), lay),
            pl.BlockSpec((None, 1, d_conv), lay),
            pl.BlockSpec((None, 1, d_conv), lay),
            pl.BlockSpec((None, n_groups, pool_group, pool_group), lambda bi, si: (l, 0, 0, 0)),
            pl.BlockSpec((None, n_groups, pool_group), lay),
            pl.BlockSpec((None, 1, d_pool), lay),
            pl.BlockSpec((None, d_conv + d_pool, d), lay),
        ],
        out_specs=pl.BlockSpec((None, ts, d), lambda bi, si: (bi, si, 0)),
        out_shape=jax.ShapeDtypeStruct((b, s, d), F32),
        scratch_shapes=[
            pltpu.VMEM((HALO + ts, d_conv), F32),
            pltpu.VMEM((HALO + ts, d_pool), F32),
            pltpu.VMEM((ts, d_conv), BF16),
            pltpu.VMEM((ts, d_pool), BF16),
        ],
        compiler_params=pltpu.CompilerParams(
            dimension_semantics=("arbitrary", "arbitrary"), vmem_limit_bytes=VMEM_LIMIT),
        name="mixer",
    )(x, mod, n1g, w_in, conv_w, conv_b, ln_g, ln_b, pool_w, pool_b, pool_scale, w_out)


def _route_kernel(x_ref, mod_ref, n2g_ref, wq_ref, keys_ref, h_ref, i1_ref, i2_ref, gate_ref,
                  q_scr, tv_scr, ti_scr, e_scr, g_scr, *, tt, heads, n_keys):
    k = PEER_TOPK
    x = x_ref[...]
    h = _rms_modulate(x, n2g_ref[...], mod_ref[3:4, :], mod_ref[4:5, :]).astype(BF16)
    h_ref[...] = h
    q = jnp.dot(h, wq_ref[...], preferred_element_type=F32).astype(BF16)
    half = keys_ref.shape[-1]
    for hp in range(2 * heads):
        q_scr[hp] = q[:, hp * half:(hp + 1) * half]

    key_id = lax.broadcasted_iota(jnp.int32, (n_keys, tt), 0).astype(F32)

    def stage1(hd, carry):
        scs = [lax.dot_general(keys_ref[2 * hd + p], q_scr[2 * hd + p], (((1,), (1,)), ((), ())),
                               preferred_element_type=F32) for p in range(2)]
        for j in range(k):
            for p in range(2):
                sc = scs[p]
                m = jnp.max(sc, axis=0, keepdims=True)
                idx = jnp.min(jnp.where(sc == m, key_id, float(n_keys)), axis=0, keepdims=True)
                scs[p] = jnp.where(key_id == idx, NEG_INF, sc)
                tv_scr[2 * hd + p, j:j + 1, :] = m
                ti_scr[2 * hd + p, j:j + 1, :] = idx
        return carry

    lax.fori_loop(0, heads, stage1, 0)

    b_full = lax.broadcasted_iota(jnp.int32, (k, tt), 0).astype(F32)
    b_half = lax.broadcasted_iota(jnp.int32, (k // 2, tt), 0).astype(F32)

    def stage2(hd, carry):
        tv1 = tv_scr[2 * hd]
        tv2 = tv_scr[2 * hd + 1]
        ti1 = ti_scr[2 * hd]
        ti2 = ti_scr[2 * hd + 1]
        vals = [tv1[0:1, :] + tv2]
        flat = [b_full]
        pay = [ti1[0:1, :] * float(n_keys) + ti2]
        for a in range(1, k // 2):
            v = tv1[a:a + 1, :] + tv2[0:k // 2, :]
            lim = k // (a + 1)
            if lim < k // 2:
                v = jnp.where(b_half < float(lim), v, NEG_INF)
            vals.append(v)
            flat.append(b_half + float(a * k))
            pay.append(ti1[a:a + 1, :] * float(n_keys) + ti2[0:k // 2, :])
        vals.append(tv1[k // 2:, :] + tv2[0:1, :])
        flat.append((b_half + float(k // 2)) * float(k))
        pay.append(ti1[k // 2:, :] * float(n_keys) + ti2[0:1, :])
        cand = jnp.concatenate(vals, axis=0)
        flat = jnp.concatenate(flat, axis=0)
        pay = jnp.concatenate(pay, axis=0)
        best = []
        for j in range(k):
            m = jnp.max(cand, axis=0, keepdims=True)
            sel = jnp.min(jnp.where(cand == m, flat, float(k * k)), axis=0, keepdims=True)
            hit = flat == sel
            e_scr[pl.ds(hd * k + j, 1), :] = jnp.max(jnp.where(hit, pay, -1.0), axis=0, keepdims=True)
            cand = jnp.where(hit, NEG_INF, cand)
            best.append(m)
        ex = [jnp.exp(m - best[0]) for m in best]
        den = ex[0]
        for j in range(1, k):
            den = den + ex[j]
        for j in range(k):
            g_scr[pl.ds(hd * k + j, 1), :] = ex[j] / den
        return carry

    lax.fori_loop(0, heads, stage2, 0)

    e = e_scr[...].T.astype(jnp.int32)
    key_bits = n_keys.bit_length() - 1
    i1_ref[...] = lax.shift_right_logical(e, key_bits)
    i2_ref[...] = lax.bitwise_and(e, n_keys - 1)
    gate_ref[...] = g_scr[...].T


def _route(x2, mod, n2g, wq, keys, l, seq):
    t, d = x2.shape
    heads, _, n_keys, half = keys.shape[1:]
    m = heads * PEER_TOPK
    tt = 256
    assert seq % tt == 0 and m == LANES and n_keys == LANES and n_keys & (n_keys - 1) == 0
    keys2 = keys.reshape(keys.shape[0], 2 * heads, n_keys, half)
    kern = functools.partial(_route_kernel, tt=tt, heads=heads, n_keys=n_keys)
    tok = lambda i: (i, 0)
    return pl.pallas_call(
        kern,
        grid=(t // tt,),
        in_specs=[
            pl.BlockSpec((tt, d), tok),
            pl.BlockSpec((None, None, 6, d), lambda i: (l, (i * tt) // seq, 0, 0)),
            pl.BlockSpec((None, 1, d), lambda i: (l, 0, 0)),
            pl.BlockSpec((None, d, 2 * heads * half), lambda i: (l, 0, 0)),
            pl.BlockSpec((None, 2 * heads, n_keys, half), lambda i: (l, 0, 0, 0)),
        ],
        out_specs=[pl.BlockSpec((tt, d), tok), pl.BlockSpec((tt, m), tok),
                   pl.BlockSpec((tt, m), tok), pl.BlockSpec((tt, m), tok)],
        out_shape=[jax.ShapeDtypeStruct((t, d), BF16), jax.ShapeDtypeStruct((t, m), jnp.int32),
                   jax.ShapeDtypeStruct((t, m), jnp.int32), jax.ShapeDtypeStruct((t, m), F32)],
        scratch_shapes=[
            pltpu.VMEM((2 * heads, tt, half), BF16),
            pltpu.VMEM((2 * heads, PEER_TOPK, tt), F32),
            pltpu.VMEM((2 * heads, PEER_TOPK, tt), F32),
            pltpu.VMEM((m, tt), F32),
            pltpu.VMEM((m, tt), F32),
        ],
        compiler_params=pltpu.CompilerParams(
            dimension_semantics=("arbitrary",), vmem_limit_bytes=VMEM_LIMIT),
        name="route",
    )(x2, mod, n2g, wq, keys2)


def _expert_in_kernel(h_ref, ut_ref, i1_ref, i2_ref, a_ref, *, te):
    j = pl.program_id(1)

    @pl.when(j == 0)
    def _():
        a_ref[...] = jnp.zeros(a_ref.shape, F32)

    a_all = jnp.dot(h_ref[...], ut_ref[...], preferred_element_type=F32)
    i1 = i1_ref[...]
    i2 = i2_ref[...]
    acc = a_ref[...]
    for c in range(te // LANES):
        picked = jnp.take_along_axis(a_all[:, c * LANES:(c + 1) * LANES], i2, axis=1)
        acc = jnp.where(i1 == j * (te // LANES) + c, picked, acc)
    a_ref[...] = acc


def _expert_in(h2, ut, i1, i2, l):
    t, d = h2.shape
    n_exp = ut.shape[-1]
    m = i1.shape[1]
    tm, te = min(512, t), 2048
    assert t % tm == 0 and n_exp % te == 0
    tok = lambda i, j: (i, 0)
    return pl.pallas_call(
        functools.partial(_expert_in_kernel, te=te),
        grid=(t // tm, n_exp // te),
        in_specs=[
            pl.BlockSpec((tm, d), tok),
            pl.BlockSpec((None, d, te), lambda i, j: (l, 0, j)),
            pl.BlockSpec((tm, m), tok),
            pl.BlockSpec((tm, m), tok),
        ],
        out_specs=pl.BlockSpec((tm, m), tok),
        out_shape=jax.ShapeDtypeStruct((t, m), F32),
        compiler_params=pltpu.CompilerParams(
            dimension_semantics=("arbitrary", "arbitrary"), vmem_limit_bytes=VMEM_LIMIT),
        name="expert_in",
    )(h2, ut, i1, i2)


def _expert_out_kernel(a_ref, gate_ref, i1_ref, i2_ref, v_ref, x_ref, mod_ref, fg_ref, o_ref,
                       w_scr, pbuf, lhs_scr, acc_scr, *, tm, te, n_keys, final_norm):
    j = pl.program_id(1)
    nk = te // LANES

    @pl.when(j == 0)
    def _():
        a = a_ref[...]
        w_scr[...] = gate_ref[...] * (0.5 * a * (1.0 + lax.erf(a * (2.0 ** -0.5))))
        acc_scr[...] = jnp.zeros(acc_scr.shape, F32)
        key_id = lax.broadcasted_iota(jnp.int32, (n_keys, LANES), 0)

        def place(t, carry):
            w_row = w_scr[pl.ds(t, 1), :]
            left = jnp.where(key_id == i1_ref[pl.ds(t, 1), :], w_row, 0.0).astype(BF16)
            right = jnp.where(key_id == i2_ref[pl.ds(t, 1), :], 1.0, 0.0).astype(BF16)
            pbuf[pl.ds(pl.multiple_of(t * P_PITCH, 8), n_keys), :] = lax.dot_general(
                left, right, (((1,), (1,)), ((), ())), preferred_element_type=F32)
            return carry

        lax.fori_loop(0, tm, place, 0, unroll=32)

    for c in range(nk):
        lhs_scr[:, c * LANES:(c + 1) * LANES] = pbuf[pl.ds(j * nk + c, tm, stride=P_PITCH), :].astype(BF16)
    acc_scr[...] += jnp.dot(lhs_scr[...], v_ref[...], preferred_element_type=F32)

    @pl.when(j == pl.num_programs(1) - 1)
    def _():
        y = x_ref[...] + mod_ref[5:6, :] * acc_scr[...]
        if final_norm:
            ms = jnp.mean(y * y, axis=-1, keepdims=True)
            y = y * lax.rsqrt(ms + EPS) * fg_ref[...]
        o_ref[...] = y


def _expert_out(a_sel, gate, i1, i2, vb, x2, mod, final_g, l, seq, final_norm):
    t, d = x2.shape
    n_exp = vb.shape[1]
    m = i1.shape[1]
    n_keys = LANES
    tm, te = 256, 2048
    assert t % tm == 0 and seq % tm == 0 and n_exp % te == 0 and n_exp == n_keys * n_keys
    tok = lambda i, j: (i, 0)
    kern = functools.partial(_expert_out_kernel, tm=tm, te=te, n_keys=n_keys, final_norm=final_norm)
    return pl.pallas_call(
        kern,
        grid=(t // tm, n_exp // te),
        in_specs=[
            pl.BlockSpec((tm, m), tok),
            pl.BlockSpec((tm, m), tok),
            pl.BlockSpec((tm, m), tok),
            pl.BlockSpec((tm, m), tok),
            pl.BlockSpec((None, te, d), lambda i, j: (l, j, 0)),
            pl.BlockSpec((tm, d), tok),
            pl.BlockSpec((None, None, 6, d), lambda i, j: (l, (i * tm) // seq, 0, 0)),
            pl.BlockSpec((1, d), lambda i, j: (0, 0)),
        ],
        out_specs=pl.BlockSpec((tm, d), tok),
        out_shape=jax.ShapeDtypeStruct((t, d), F32),
        scratch_shapes=[
            pltpu.VMEM((tm, m), F32),
            pltpu.VMEM((tm * P_PITCH, LANES), F32),
            pltpu.VMEM((tm, te), BF16),
            pltpu.VMEM((tm, d), F32),
        ],
        compiler_params=pltpu.CompilerParams(
            dimension_semantics=("arbitrary", "arbitrary"), vmem_limit_bytes=VMEM_LIMIT),
        name="expert_out",
    )(a_sel, gate, i1, i2, vb, x2, mod, final_g)


def kernel(x, c, w_ada, b_ada, norm1_g, norm2_g, w_in, w_out, conv_w, conv_b, conv_ln_g, conv_ln_b,
           pool_w, pool_b, pool_scale, peer_wq, peer_keys, peer_u, peer_v, final_g):
    depth = w_ada.shape[0]
    b, s, d = x.shape
    row = lambda p: p.reshape(depth, 1, p.shape[-1])

    mod = _adaln(c, w_ada, b_ada).reshape(depth, b, 6, d)
    w_in_b = w_in.astype(BF16)
    w_out_b = w_out.astype(BF16)
    pool_w_b = pool_w.astype(BF16)
    wq_b = peer_wq.astype(BF16)
    keys_b = peer_keys.astype(BF16)
    ut_b = jnp.swapaxes(peer_u, 1, 2).astype(BF16)
    v_b = peer_v.astype(BF16)
    n1g, n2g = row(norm1_g), row(norm2_g)
    cb, lg, lb, ps = row(conv_b), row(conv_ln_g), row(conv_ln_b), row(pool_scale)
    fg = final_g.reshape(1, d)

    for l in range(depth):
        x = _mixer(x, mod, n1g, w_in_b, conv_w, cb, lg, lb, pool_w_b, pool_b, ps, w_out_b, l)
        x2 = x.reshape(b * s, d)
        h2, i1, i2, gate = _route(x2, mod, n2g, wq_b, keys_b, l, s)
        a_sel = _expert_in(h2, ut_b, i1, i2, l)
        x2 = _expert_out(a_sel, gate, i1, i2, v_b, x2, mod, fg, l, s, l == depth - 1)
        x = x2.reshape(b, s, d)
    return x
```

```python
import functools

import jax
import jax.numpy as jnp
from jax import lax
from jax.experimental import pallas as pl
from jax.experimental.pallas import tpu as pltpu

EPS = 1e-6
POOL_WINDOWS = (2, 4, 8, 16)
PEER_TOPK = 16
LANES = 128
HALO = 32
CONV_ROWS = 32
P_PITCH = 136
VMEM_LIMIT = 56 * 1024 * 1024

F32 = jnp.float32
BF16 = jnp.bfloat16
NEG_INF = float("-inf")


def _rms_modulate(x, gain, shift, scale):
    ms = jnp.mean(x * x, axis=-1, keepdims=True)
    return (x * lax.rsqrt(ms + EPS) * gain) * (1.0 + scale) + shift


def _adaln_kernel(c_ref, w_ref, b_ref, o_ref):
    c = c_ref[...]
    cond = c * jax.nn.sigmoid(c)
    o_ref[...] = jnp.dot(cond, w_ref[...], preferred_element_type=F32,
                         precision=lax.Precision.HIGHEST) + b_ref[...]


def _adaln(c, w_ada, b_ada):
    depth, d, n = w_ada.shape
    b = c.shape[0]
    tn = 1536
    return pl.pallas_call(
        _adaln_kernel,
        grid=(depth, n // tn),
        in_specs=[
            pl.BlockSpec((b, d), lambda l, j: (0, 0)),
            pl.BlockSpec((None, d, tn), lambda l, j: (l, 0, j)),
            pl.BlockSpec((None, 1, tn), lambda l, j: (l, 0, j)),
        ],
        out_specs=pl.BlockSpec((None, b, tn), lambda l, j: (l, 0, j)),
        out_shape=jax.ShapeDtypeStruct((depth, b, n), F32),
        compiler_params=pltpu.CompilerParams(vmem_limit_bytes=VMEM_LIMIT),
        name="adaln",
    )(c, w_ada, b_ada.reshape(depth, 1, n))


def _mixer_kernel(x_ref, mod_ref, n1g_ref, win_ref, convw_ref, convb_ref, lng_ref, lnb_ref,
                  poolw_ref, poolb_ref, pscale_ref, wout_ref, o_ref,
                  ubuf, zbuf, ya_scr, yb_scr, *, ts, d_conv, conv_width):
    s = pl.program_id(1)
    pool_group = poolw_ref.shape[-1]

    @pl.when(s == 0)
    def _():
        ubuf[0:HALO, :] = jnp.zeros((HALO, ubuf.shape[1]), F32)
        zbuf[0:HALO, :] = jnp.zeros((HALO, zbuf.shape[1]), F32)

    x = x_ref[...]
    h = _rms_modulate(x, n1g_ref[...], mod_ref[0:1, :], mod_ref[1:2, :])
    z = jnp.dot(h.astype(BF16), win_ref[...], preferred_element_type=F32)
    ubuf[HALO:HALO + ts, :] = z[:, :d_conv] * jax.nn.sigmoid(z[:, d_conv:2 * d_conv])
    zbuf[HALO:HALO + ts, :] = z[:, 2 * d_conv:]

    convb = convb_ref[...]
    lng = lng_ref[...]
    lnb = lnb_ref[...]
    for r0 in range(0, ts, CONV_ROWS):
        acc = jnp.broadcast_to(convb, (CONV_ROWS, d_conv))
        for k in range(conv_width):
            start = HALO - (conv_width - 1) + k + r0
            acc = acc + convw_ref[k:k + 1, :] * ubuf[start:start + CONV_ROWS, :]
        mu = jnp.mean(acc, axis=-1, keepdims=True)
        cen = acc - mu
        var = jnp.mean(cen * cen, axis=-1, keepdims=True)
        yn = cen * lax.rsqrt(var + EPS) * lng + lnb
        ya_scr[r0:r0 + CONV_ROWS, :] = (yn * jax.nn.sigmoid(yn)).astype(BF16)

    frame = (s * ts + lax.broadcasted_iota(jnp.int32, (ts, 1), 0) + 1).astype(F32)
    for g, w in enumerate(POOL_WINDOWS):
        lo = g * pool_group
        zg = zbuf[HALO:HALO + ts, lo:lo + pool_group]
        tot = zg
        for j in range(1, w):
            tot = tot + zbuf[HALO - j:HALO - j + ts, lo:lo + pool_group]
        p = tot / jnp.minimum(frame, float(w)) - zg
        yg = jnp.dot(p.astype(BF16), poolw_ref[g], preferred_element_type=F32) + poolb_ref[g:g + 1, :]
        yb_scr[:, lo:lo + pool_group] = (yg * pscale_ref[:, lo:lo + pool_group]).astype(BF16)

    mix = jnp.dot(ya_scr[...], wout_ref[0:d_conv, :], preferred_element_type=F32)
    mix = mix + jnp.dot(yb_scr[...], wout_ref[d_conv:, :], preferred_element_type=F32)
    o_ref[...] = x + mod_ref[2:3, :] * mix

    ubuf[0:HALO, :] = ubuf[ts:ts + HALO, :]
    zbuf[0:HALO, :] = zbuf[ts:ts + HALO, :]


def _mixer(x, mod, n1g, w_in, conv_w, conv_b, ln_g, ln_b, pool_w, pool_b, pool_scale, w_out, l):
    b, s, d = x.shape
    d_in = w_in.shape[-1]
    conv_width, d_conv = conv_w.shape[1:]
    n_groups, pool_group = pool_w.shape[1:3]
    d_pool = n_groups * pool_group
    ts = min(256, s)
    assert s % ts == 0 and ts % CONV_ROWS == 0 and ts >= HALO
    assert HALO >= conv_width - 1 and HALO >= max(POOL_WINDOWS) - 1
    assert d_in == 2 * d_conv + d_pool and n_groups == len(POOL_WINDOWS)
    lay = lambda bi, si: (l, 0, 0)
    kern = functools.partial(_mixer_kernel, ts=ts, d_conv=d_conv, conv_width=conv_width)
    return pl.pallas_call(
        kern,
        grid=(b, s // ts),
        in_specs=[
            pl.BlockSpec((None, ts, d), lambda bi, si: (bi, si, 0)),
            pl.BlockSpec((None, None, 6, d), lambda bi, si: (l, bi, 0, 0)),
            pl.BlockSpec((None, 1, d), lay),
            pl.BlockSpec((None, d, d_in), lay),
            pl.BlockSpec((None, conv_width, d_conv), lay),
            pl.BlockSpec((None, 1, d_conv), lay),
            pl.BlockSpec((None, 1, d_conv), lay),
            pl.BlockSpec((None, 1, d_conv), lay),
            pl.BlockSpec((None, n_groups, pool_group, pool_group), lambda bi, si: (l, 0, 0, 0)),
            pl.BlockSpec((None, n_groups, pool_group), lay),
            pl.BlockSpec((None, 1, d_pool), lay),
            pl.BlockSpec((None, d_conv + d_pool, d), lay),
        ],
        out_specs=pl.BlockSpec((None, ts, d), lambda bi, si: (bi, si, 0)),
        out_shape=jax.ShapeDtypeStruct((b, s, d), F32),
        scratch_shapes=[
            pltpu.VMEM((HALO + ts, d_conv), F32),
            pltpu.VMEM((HALO + ts, d_pool), F32),
            pltpu.VMEM((ts, d_conv), BF16),
            pltpu.VMEM((ts, d_pool), BF16),
        ],
        compiler_params=pltpu.CompilerParams(
            dimension_semantics=("arbitrary", "arbitrary"), vmem_limit_bytes=VMEM_LIMIT),
        name="mixer",
    )(x, mod, n1g, w_in, conv_w, conv_b, ln_g, ln_b, pool_w, pool_b, pool_scale, w_out)


def _route_kernel(x_ref, mod_ref, n2g_ref, wq_ref, keys_ref, h_ref, i1_ref, i2_ref, gate_ref,
                  q_scr, tv_scr, ti_scr, e_scr, g_scr, *, tt, heads, n_keys):
    k = PEER_TOPK
    x = x_ref[...]
    h = _rms_modulate(x, n2g_ref[...], mod_ref[3:4, :], mod_ref[4:5, :]).astype(BF16)
    h_ref[...] = h
    q = jnp.dot(h, wq_ref[...], preferred_element_type=F32).astype(BF16)
    half = keys_ref.shape[-1]
    for hp in range(2 * heads):
        q_scr[hp] = q[:, hp * half:(hp + 1) * half]

    key_id = lax.broadcasted_iota(jnp.int32, (n_keys, tt), 0).astype(F32)

    def stage1(hd, carry):
        scs = [lax.dot_general(keys_ref[2 * hd + p], q_scr[2 * hd + p], (((1,), (1,)), ((), ())),
                               preferred_element_type=F32) for p in range(2)]
        for j in range(k):
            for p in range(2):
                sc = scs[p]
                m = jnp.max(sc, axis=0, keepdims=True)
                idx = jnp.min(jnp.where(sc == m, key_id, float(n_keys)), axis=0, keepdims=True)
                scs[p] = jnp.where(key_id == idx, NEG_INF, sc)
                tv_scr[2 * hd + p, j:j + 1, :] = m
                ti_scr[2 * hd + p, j:j + 1, :] = idx
        return carry

    lax.fori_loop(0, heads, stage1, 0)

    b_full = lax.broadcasted_iota(jnp.int32, (k, tt), 0).astype(F32)
    b_half = lax.broadcasted_iota(jnp.int32, (k // 2, tt), 0).astype(F32)

    def stage2(hd, carry):
        tv1 = tv_scr[2 * hd]
        tv2 = tv_scr[2 * hd + 1]
        ti1 = ti_scr[2 * hd]
        ti2 = ti_scr[2 * hd + 1]
        vals = [tv1[0:1, :] + tv2]
        flat = [b_full]
        pay = [ti1[0:1, :] * float(n_keys) + ti2]
        for a in range(1, k // 2):
            v = tv1[a:a + 1, :] + tv2[0:k // 2, :]
            lim = k // (a + 1)
            if lim < k // 2:
                v = jnp.where(b_half < float(lim), v, NEG_INF)
            vals.append(v)
            flat.append(b_half + float(a * k))
            pay.append(ti1[a:a + 1, :] * float(n_keys) + ti2[0:k // 2, :])
        vals.append(tv1[k // 2:, :] + tv2[0:1, :])
        flat.append((b_half + float(k // 2)) * float(k))
        pay.append(ti1[k // 2:, :] * float(n_keys) + ti2[0:1, :])
        cand = jnp.concatenate(vals, axis=0)
        flat = jnp.concatenate(flat, axis=0)
        pay = jnp.concatenate(pay, axis=0)
        best = []
        for j in range(k):
            m = jnp.max(cand, axis=0, keepdims=True)
            sel = jnp.min(jnp.where(cand == m, flat, float(k * k)), axis=0, keepdims=True)
            hit = flat == sel
            e_scr[pl.ds(hd * k + j, 1), :] = jnp.max(jnp.where(hit, pay, -1.0), axis=0, keepdims=True)
            cand = jnp.where(hit, NEG_INF, cand)
            best.append(m)
        ex = [jnp.exp(m - best[0]) for m in best]
        den = ex[0]
        for j in range(1, k):
            den = den + ex[j]
        for j in range(k):
            g_scr[pl.ds(hd * k + j, 1), :] = ex[j] / den
        return carry

    lax.fori_loop(0, heads, stage2, 0)

    e = e_scr[...].T.astype(jnp.int32)
    key_bits = n_keys.bit_length() - 1
    i1_ref[...] = lax.shift_right_logical(e, key_bits)
    i2_ref[...] = lax.bitwise_and(e, n_keys - 1)
    gate_ref[...] = g_scr[...].T


def _route(x2, mod, n2g, wq, keys, l, seq):
    t, d = x2.shape
    heads, _, n_keys, half = keys.shape[1:]
    m = heads * PEER_TOPK
    tt = 256
    assert seq % tt == 0 and m == LANES and n_keys == LANES and n_keys & (n_keys - 1) == 0
    keys2 = keys.reshape(keys.shape[0], 2 * heads, n_keys, half)
    kern = functools.partial(_route_kernel, tt=tt, heads=heads, n_keys=n_keys)
    tok = lambda i: (i, 0)
    return pl.pallas_call(
        kern,
        grid=(t // tt,),
        in_specs=[
            pl.BlockSpec((tt, d), tok),
            pl.BlockSpec((None, None, 6, d), lambda i: (l, (i * tt) // seq, 0, 0)),
            pl.BlockSpec((None, 1, d), lambda i: (l, 0, 0)),
            pl.BlockSpec((None, d, 2 * heads * half), lambda i: (l, 0, 0)),
            pl.BlockSpec((None, 2 * heads, n_keys, half), lambda i: (l, 0, 0, 0)),
        ],
        out_specs=[pl.BlockSpec((tt, d), tok), pl.BlockSpec((tt, m), tok),
                   pl.BlockSpec((tt, m), tok), pl.BlockSpec((tt, m), tok)],
        out_shape=[jax.ShapeDtypeStruct((t, d), BF16), jax.ShapeDtypeStruct((t, m), jnp.int32),
                   jax.ShapeDtypeStruct((t, m), jnp.int32), jax.ShapeDtypeStruct((t, m), F32)],
        scratch_shapes=[
            pltpu.VMEM((2 * heads, tt, half), BF16),
            pltpu.VMEM((2 * heads, PEER_TOPK, tt), F32),
            pltpu.VMEM((2 * heads, PEER_TOPK, tt), F32),
            pltpu.VMEM((m, tt), F32),
            pltpu.VMEM((m, tt), F32),
        ],
        compiler_params=pltpu.CompilerParams(
            dimension_semantics=("arbitrary",), vmem_limit_bytes=VMEM_LIMIT),
        name="route",
    )(x2, mod, n2g, wq, keys2)


def _expert_in_kernel(h_ref, ut_ref, i1_ref, i2_ref, a_ref, *, te):
    j = pl.program_id(1)

    @pl.when(j == 0)
    def _():
        a_ref[...] = jnp.zeros(a_ref.shape, F32)

    a_all = jnp.dot(h_ref[...], ut_ref[...], preferred_element_type=F32)
    i1 = i1_ref[...]
    i2 = i2_ref[...]
    acc = a_ref[...]
    for c in range(te // LANES):
        picked = jnp.take_along_axis(a_all[:, c * LANES:(c + 1) * LANES], i2, axis=1)
        acc = jnp.where(i1 == j * (te // LANES) + c, picked, acc)
    a_ref[...] = acc


def _expert_in(h2, ut, i1, i2, l):
    t, d = h2.shape
    n_exp = ut.shape[-1]
    m = i1.shape[1]
    tm, te = min(512, t), 2048
    assert t % tm == 0 and n_exp % te == 0
    tok = lambda i, j: (i, 0)
    return pl.pallas_call(
        functools.partial(_expert_in_kernel, te=te),
        grid=(t // tm, n_exp // te),
        in_specs=[
            pl.BlockSpec((tm, d), tok),
            pl.BlockSpec((None, d, te), lambda i, j: (l, 0, j)),
            pl.BlockSpec((tm, m), tok),
            pl.BlockSpec((tm, m), tok),
        ],
        out_specs=pl.BlockSpec((tm, m), tok),
        out_shape=jax.ShapeDtypeStruct((t, m), F32),
        compiler_params=pltpu.CompilerParams(
            dimension_semantics=("arbitrary", "arbitrary"), vmem_limit_bytes=VMEM_LIMIT),
        name="expert_in",
    )(h2, ut, i1, i2)


def _expert_out_kernel(a_ref, gate_ref, i1_ref, i2_ref, v_ref, x_ref, mod_ref, fg_ref, o_ref,
                       w_scr, pbuf, lhs_scr, acc_scr, *, tm, te, n_keys, final_norm):
    j = pl.program_id(1)
    nk = te // LANES

    @pl.when(j == 0)
    def _():
        a = a_ref[...]
        w_scr[...] = gate_ref[...] * (0.5 * a * (1.0 + lax.erf(a * (2.0 ** -0.5))))
        acc_scr[...] = jnp.zeros(acc_scr.shape, F32)
        key_id = lax.broadcasted_iota(jnp.int32, (n_keys, LANES), 0)

        def place(t, carry):
            w_row = w_scr[pl.ds(t, 1), :]
            left = jnp.where(key_id == i1_ref[pl.ds(t, 1), :], w_row, 0.0).astype(BF16)
            right = jnp.where(key_id == i2_ref[pl.ds(t, 1), :], 1.0, 0.0).astype(BF16)
            pbuf[pl.ds(pl.multiple_of(t * P_PITCH, 8), n_keys), :] = lax.dot_general(
                left, right, (((1,), (1,)), ((), ())), preferred_element_type=F32)
            return carry

        lax.fori_loop(0, tm, place, 0, unroll=32)

    for c in range(nk):
        lhs_scr[:, c * LANES:(c + 1) * LANES] = pbuf[pl.ds(j * nk + c, tm, stride=P_PITCH), :].astype(BF16)
    acc_scr[...] += jnp.dot(lhs_scr[...], v_ref[...], preferred_element_type=F32)

    @pl.when(j == pl.num_programs(1) - 1)
    def _():
        y = x_ref[...] + mod_ref[5:6, :] * acc_scr[...]
        if final_norm:
            ms = jnp.mean(y * y, axis=-1, keepdims=True)
            y = y * lax.rsqrt(ms + EPS) * fg_ref[...]
        o_ref[...] = y


def _expert_out(a_sel, gate, i1, i2, vb, x2, mod, final_g, l, seq, final_norm):
    t, d = x2.shape
    n_exp = vb.shape[1]
    m = i1.shape[1]
    n_keys = LANES
    tm, te = 512, 1024
    assert t % tm == 0 and seq % tm == 0 and n_exp % te == 0 and n_exp == n_keys * n_keys
    tok = lambda i, j: (i, 0)
    kern = functools.partial(_expert_out_kernel, tm=tm, te=te, n_keys=n_keys, final_norm=final_norm)
    return pl.pallas_call(
        kern,
        grid=(t // tm, n_exp // te),
        in_specs=[
            pl.BlockSpec((tm, m), tok),
            pl.BlockSpec((tm, m), tok),
            pl.BlockSpec((tm, m), tok),
            pl.BlockSpec((tm, m), tok),
            pl.BlockSpec((None, te, d), lambda i, j: (l, j, 0)),
            pl.BlockSpec((tm, d), tok),
            pl.BlockSpec((None, None, 6, d), lambda i, j: (l, (i * tm) // seq, 0, 0)),
            pl.BlockSpec((1, d), lambda i, j: (0, 0)),
        ],
        out_specs=pl.BlockSpec((tm, d), tok),
        out_shape=jax.ShapeDtypeStruct((t, d), F32),
        scratch_shapes=[
            pltpu.VMEM((tm, m), F32),
            pltpu.VMEM((tm * P_PITCH, LANES), F32),
            pltpu.VMEM((tm, te), BF16),
            pltpu.VMEM((tm, d), F32),
        ],
        compiler_params=pltpu.CompilerParams(
            dimension_semantics=("arbitrary", "arbitrary"), vmem_limit_bytes=VMEM_LIMIT),
        name="expert_out",
    )(a_sel, gate, i1, i2, vb, x2, mod, final_g)


def kernel(x, c, w_ada, b_ada, norm1_g, norm2_g, w_in, w_out, conv_w, conv_b, conv_ln_g, conv_ln_b,
           pool_w, pool_b, pool_scale, peer_wq, peer_keys, peer_u, peer_v, final_g):
    depth = w_ada.shape[0]
    b, s, d = x.shape
    row = lambda p: p.reshape(depth, 1, p.shape[-1])

    mod = _adaln(c, w_ada, b_ada).reshape(depth, b, 6, d)
    w_in_b = w_in.astype(BF16)
    w_out_b = w_out.astype(BF16)
    pool_w_b = pool_w.astype(BF16)
    wq_b = peer_wq.astype(BF16)
    keys_b = peer_keys.astype(BF16)
    ut_b = jnp.swapaxes(peer_u, 1, 2).astype(BF16)
    v_b = peer_v.astype(BF16)
    n1g, n2g = row(norm1_g), row(norm2_g)
    cb, lg, lb, ps = row(conv_b), row(conv_ln_g), row(conv_ln_b), row(pool_scale)
    fg = final_g.reshape(1, d)

    for l in range(depth):
        x = _mixer(x, mod, n1g, w_in_b, conv_w, cb, lg, lb, pool_w_b, pool_b, ps, w_out_b, l)
        x2 = x.reshape(b * s, d)
        h2, i1, i2, gate = _route(x2, mod, n2g, wq_b, keys_b, l, s)
        a_sel = _expert_in(h2, ut_b, i1, i2, l)
        x2 = _expert_out(a_sel, gate, i1, i2, v_b, x2, mod, fg, l, s, l == depth - 1)
        x = x2.reshape(b, s, d)
    return x
```

```python
import functools

import jax
import jax.numpy as jnp
from jax import lax
from jax.experimental import pallas as pl
from jax.experimental.pallas import tpu as pltpu

EPS = 1e-6
POOL_WINDOWS = (2, 4, 8, 16)
PEER_TOPK = 16
LANES = 128
HALO = 32
CONV_ROWS = 32
P_PITCH = 136
VMEM_LIMIT = 58 * 1024 * 1024

F32 = jnp.float32
BF16 = jnp.bfloat16
NEG_INF = float("-inf")


def _rms_modulate(x, gain, shift, scale):
    ms = jnp.mean(x * x, axis=-1, keepdims=True)
    return (x * lax.rsqrt(ms + EPS) * gain) * (1.0 + scale) + shift


def _adaln_kernel(c_ref, w_ref, b_ref, o_ref):
    c = c_ref[...]
    cond = c * jax.nn.sigmoid(c)
    o_ref[...] = jnp.dot(cond, w_ref[...], preferred_element_type=F32,
                         precision=lax.Precision.HIGHEST) + b_ref[...]


def _adaln(c, w_ada, b_ada):
    depth, d, n = w_ada.shape
    b = c.shape[0]
    tn = 1536
    return pl.pallas_call(
        _adaln_kernel,
        grid=(depth, n // tn),
        in_specs=[
            pl.BlockSpec((b, d), lambda l, j: (0, 0)),
            pl.BlockSpec((None, d, tn), lambda l, j: (l, 0, j)),
            pl.BlockSpec((None, 1, tn), lambda l, j: (l, 0, j)),
        ],
        out_specs=pl.BlockSpec((None, b, tn), lambda l, j: (l, 0, j)),
        out_shape=jax.ShapeDtypeStruct((depth, b, n), F32),
        compiler_params=pltpu.CompilerParams(vmem_limit_bytes=VMEM_LIMIT),
        name="adaln",
    )(c, w_ada, b_ada.reshape(depth, 1, n))


def _mixer_kernel(x_ref, mod_ref, n1g_ref, win_ref, convw_ref, convb_ref, lng_ref, lnb_ref,
                  poolw_ref, poolb_ref, pscale_ref, wout_ref, o_ref,
                  ubuf, zbuf, ya_scr, yb_scr, *, ts, d_conv, conv_width):
    s = pl.program_id(1)
    pool_group = poolw_ref.shape[-1]

    @pl.when(s == 0)
    def _():
        ubuf[0:HALO, :] = jnp.zeros((HALO, ubuf.shape[1]), F32)
        zbuf[0:HALO, :] = jnp.zeros((HALO, zbuf.shape[1]), F32)

    x = x_ref[...]
    h = _rms_modulate(x, n1g_ref[...], mod_ref[0:1, :], mod_ref[1:2, :])
    z = jnp.dot(h.astype(BF16), win_ref[...], preferred_element_type=F32)
    ubuf[HALO:HALO + ts, :] = z[:, :d_conv] * jax.nn.sigmoid(z[:, d_conv:2 * d_conv])
    zbuf[HALO:HALO + ts, :] = z[:, 2 * d_conv:]

    convb = convb_ref[...]
    lng = lng_ref[...]
    lnb = lnb_ref[...]
    for r0 in range(0, ts, CONV_ROWS):
        acc = jnp.broadcast_to(convb, (CONV_ROWS, d_conv))
        for k in range(conv_width):
            start = HALO - (conv_width - 1) + k + r0
            acc = acc + convw_ref[k:k + 1, :] * ubuf[start:start + CONV_ROWS, :]
        mu = jnp.mean(acc, axis=-1, keepdims=True)
        cen = acc - mu
        var = jnp.mean(cen * cen, axis=-1, keepdims=True)
        yn = cen * lax.rsqrt(var + EPS) * lng + lnb
        ya_scr[r0:r0 + CONV_ROWS, :] = (yn * jax.nn.sigmoid(yn)).astype(BF16)

    frame = (s * ts + lax.broadcasted_iota(jnp.int32, (ts, 1), 0) + 1).astype(F32)
    for g, w in enumerate(POOL_WINDOWS):
        lo = g * pool_group
        zg = zbuf[HALO:HALO + ts, lo:lo + pool_group]
        tot = zg
        for j in range(1, w):
            tot = tot + zbuf[HALO - j:HALO - j + ts, lo:lo + pool_group]
        p = tot / jnp.minimum(frame, float(w)) - zg
        yg = jnp.dot(p.astype(BF16), poolw_ref[g], preferred_element_type=F32) + poolb_ref[g:g + 1, :]
        yb_scr[:, lo:lo + pool_group] = (yg * pscale_ref[:, lo:lo + pool_group]).astype(BF16)

    mix = jnp.dot(ya_scr[...], wout_ref[0:d_conv, :], preferred_element_type=F32)
    mix = mix + jnp.dot(yb_scr[...], wout_ref[d_conv:, :], preferred_element_type=F32)
    o_ref[...] = x + mod_ref[2:3, :] * mix

    ubuf[0:HALO, :] = ubuf[ts:ts + HALO, :]
    zbuf[0:HALO, :] = zbuf[ts:ts + HALO, :]


def _mixer(x, mod, n1g, w_in, conv_w, conv_b, ln_g, ln_b, pool_w, pool_b, pool_scale, w_out, l):
    b, s, d = x.shape
    d_in = w_in.shape[-1]
    conv_width, d_conv = conv_w.shape[1:]
    n_groups, pool_group = pool_w.shape[1:3]
    d_pool = n_groups * pool_group
    ts = min(256, s)
    assert s % ts == 0 and ts % CONV_ROWS == 0 and ts >= HALO
    assert HALO >= conv_width - 1 and HALO >= max(POOL_WINDOWS) - 1
    assert d_in == 2 * d_conv + d_pool and n_groups == len(POOL_WINDOWS)
    lay = lambda bi, si: (l, 0, 0)
    kern = functools.partial(_mixer_kernel, ts=ts, d_conv=d_conv, conv_width=conv_width)
    return pl.pallas_call(
        kern,
        grid=(b, s // ts),
        in_specs=[
            pl.BlockSpec((None, ts, d), lambda bi, si: (bi, si, 0)),
            pl.BlockSpec((None, None, 6, d), lambda bi, si: (l, bi, 0, 0)),
            pl.BlockSpec((None, 1, d), lay),
            pl.BlockSpec((None, d, d_in), lay),
            pl.BlockSpec((None, conv_width, d_conv), lay),
            pl.BlockSpec((None, 1, d_conv), lay),
            pl.BlockSpec((None, 1, d_conv), lay),
            pl.BlockSpec((None, 1, d_conv), lay),
            pl.BlockSpec((None, n_groups, pool_group, pool_group), lambda bi, si: (l, 0, 0, 0)),
            pl.BlockSpec((None, n_groups, pool_group), lay),
            pl.BlockSpec((None, 1, d_pool), lay),
            pl.BlockSpec((None, d_conv + d_pool, d), lay),
        ],
        out_specs=pl.BlockSpec((None, ts, d), lambda bi, si: (bi, si, 0)),
        out_shape=jax.ShapeDtypeStruct((b, s, d), F32),
        scratch_shapes=[
            pltpu.VMEM((HALO + ts, d_conv), F32),
            pltpu.VMEM((HALO + ts, d_pool), F32),
            pltpu.VMEM((ts, d_conv), BF16),
            pltpu.VMEM((ts, d_pool), BF16),
        ],
        compiler_params=pltpu.CompilerParams(
            dimension_semantics=("arbitrary", "arbitrary"), vmem_limit_bytes=VMEM_LIMIT),
        name="mixer",
    )(x, mod, n1g, w_in, conv_w, conv_b, ln_g, ln_b, pool_w, pool_b, pool_scale, w_out)


def _sort_network(n):
    pairs = []
    p = 1
    while p < n:
        k = p
        while k >= 1:
            for j in range(k % p, n - k, 2 * k):
                for i in range(min(k, n - j - k)):
                    if (i + j) // (2 * p) == (i + j + k) // (2 * p):
                        pairs.append((i + j, i + j + k))
            k //= 2
        p *= 2
    return pairs


def _topk_rows(sc, k, n_keys):
    nv = n_keys // 8
    tt = sc.shape[1]
    assert k <= nv and nv & (nv - 1) == 0
    sub = lax.broadcasted_iota(jnp.int32, (8, tt), 0).astype(F32)
    val = [sc[8 * i:8 * (i + 1), :] for i in range(nv)]
    idx = [sub + float(8 * i) for i in range(nv)]
    for a, b in _sort_network(nv):
        va, vb, ia, ib = val[a], val[b], idx[a], idx[b]
        swap = (vb > va) | ((vb == va) & (ib < ia))
        val[a], val[b] = jnp.maximum(va, vb), jnp.minimum(va, vb)
        idx[a], idx[b] = jnp.where(swap, ib, ia), jnp.where(swap, ia, ib)
    top_v, top_i = [], []
    for j in range(k):
        m = jnp.max(val[0], axis=0, keepdims=True)
        win = jnp.min(jnp.where(val[0] == m, idx[0], float(n_keys)), axis=0, keepdims=True)
        top_v.append(m)
        top_i.append(win)
        popped = idx[0] == win
        for i in range(nv - 1 - j):
            val[i] = jnp.where(popped, val[i + 1], val[i])
            idx[i] = jnp.where(popped, idx[i + 1], idx[i])
    return top_v, top_i


def _route_kernel(x_ref, mod_ref, n2g_ref, wq_ref, keys_ref, h_ref, i1_ref, i2_ref, gate_ref,
                  q_scr, tv_scr, ti_scr, e_scr, g_scr, *, tt, heads, n_keys):
    k = PEER_TOPK
    x = x_ref[...]
    h = _rms_modulate(x, n2g_ref[...], mod_ref[3:4, :], mod_ref[4:5, :]).astype(BF16)
    h_ref[...] = h
    q = jnp.dot(h, wq_ref[...], preferred_element_type=F32).astype(BF16)
    half = keys_ref.shape[-1]
    for hp in range(2 * heads):
        q_scr[hp] = q[:, hp * half:(hp + 1) * half]

    def stage1(hd, carry):
        for p in range(2):
            hp = 2 * hd + p
            sc = lax.dot_general(keys_ref[hp], q_scr[hp], (((1,), (1,)), ((), ())),
                                 preferred_element_type=F32)
            top_v, top_i = _topk_rows(sc, k, n_keys)
            for j in range(k):
                tv_scr[hp, j:j + 1, :] = top_v[j]
                ti_scr[hp, j:j + 1, :] = top_i[j]
        return carry

    lax.fori_loop(0, heads, stage1, 0)

    b_full = lax.broadcasted_iota(jnp.int32, (k, tt), 0).astype(F32)
    b_half = lax.broadcasted_iota(jnp.int32, (k // 2, tt), 0).astype(F32)

    def stage2(hd, carry):
        tv1 = tv_scr[2 * hd]
        tv2 = tv_scr[2 * hd + 1]
        ti1 = ti_scr[2 * hd]
        ti2 = ti_scr[2 * hd + 1]
        vals = [tv1[0:1, :] + tv2]
        flat = [b_full]
        pay = [ti1[0:1, :] * float(n_keys) + ti2]
        for a in range(1, k // 2):
            v = tv1[a:a + 1, :] + tv2[0:k // 2, :]
            lim = k // (a + 1)
            if lim < k // 2:
                v = jnp.where(b_half < float(lim), v, NEG_INF)
            vals.append(v)
            flat.append(b_half + float(a * k))
            pay.append(ti1[a:a + 1, :] * float(n_keys) + ti2[0:k // 2, :])
        vals.append(tv1[k // 2:, :] + tv2[0:1, :])
        flat.append((b_half + float(k // 2)) * float(k))
        pay.append(ti1[k // 2:, :] * float(n_keys) + ti2[0:1, :])
        cand = jnp.concatenate(vals, axis=0)
        flat = jnp.concatenate(flat, axis=0)
        pay = jnp.concatenate(pay, axis=0)
        best = []
        for j in range(k):
            m = jnp.max(cand, axis=0, keepdims=True)
            sel = jnp.min(jnp.where(cand == m, flat, float(k * k)), axis=0, keepdims=True)
            hit = flat == sel
            e_scr[pl.ds(hd * k + j, 1), :] = jnp.max(jnp.where(hit, pay, -1.0), axis=0, keepdims=True)
            cand = jnp.where(hit, NEG_INF, cand)
            best.append(m)
        ex = [jnp.exp(m - best[0]) for m in best]
        den = ex[0]
        for j in range(1, k):
            den = den + ex[j]
        for j in range(k):
            g_scr[pl.ds(hd * k + j, 1), :] = ex[j] / den
        return carry

    lax.fori_loop(0, heads, stage2, 0)

    e = e_scr[...].T.astype(jnp.int32)
    key_bits = n_keys.bit_length() - 1
    i1_ref[...] = lax.shift_right_logical(e, key_bits)
    i2_ref[...] = lax.bitwise_and(e, n_keys - 1)
    gate_ref[...] = g_scr[...].T


def _route(x2, mod, n2g, wq, keys, l, seq):
    t, d = x2.shape
    heads, _, n_keys, half = keys.shape[1:]
    m = heads * PEER_TOPK
    tt = 256
    assert seq % tt == 0 and m == LANES and n_keys == LANES and n_keys & (n_keys - 1) == 0
    keys2 = keys.reshape(keys.shape[0], 2 * heads, n_keys, half)
    kern = functools.partial(_route_kernel, tt=tt, heads=heads, n_keys=n_keys)
    tok = lambda i: (i, 0)
    return pl.pallas_call(
        kern,
        grid=(t // tt,),
        in_specs=[
            pl.BlockSpec((tt, d), tok),
            pl.BlockSpec((None, None, 6, d), lambda i: (l, (i * tt) // seq, 0, 0)),
            pl.BlockSpec((None, 1, d), lambda i: (l, 0, 0)),
            pl.BlockSpec((None, d, 2 * heads * half), lambda i: (l, 0, 0)),
            pl.BlockSpec((None, 2 * heads, n_keys, half), lambda i: (l, 0, 0, 0)),
        ],
        out_specs=[pl.BlockSpec((tt, d), tok), pl.BlockSpec((tt, m), tok),
                   pl.BlockSpec((tt, m), tok), pl.BlockSpec((tt, m), tok)],
        out_shape=[jax.ShapeDtypeStruct((t, d), BF16), jax.ShapeDtypeStruct((t, m), jnp.int32),
                   jax.ShapeDtypeStruct((t, m), jnp.int32), jax.ShapeDtypeStruct((t, m), F32)],
        scratch_shapes=[
            pltpu.VMEM((2 * heads, tt, half), BF16),
            pltpu.VMEM((2 * heads, PEER_TOPK, tt), F32),
            pltpu.VMEM((2 * heads, PEER_TOPK, tt), F32),
            pltpu.VMEM((m, tt), F32),
            pltpu.VMEM((m, tt), F32),
        ],
        compiler_params=pltpu.CompilerParams(
            dimension_semantics=("arbitrary",), vmem_limit_bytes=VMEM_LIMIT),
        name="route",
    )(x2, mod, n2g, wq, keys2)


def _expert_in_kernel(h_ref, ut_ref, i1_ref, i2_ref, a_ref, *, te):
    j = pl.program_id(1)

    @pl.when(j == 0)
    def _():
        a_ref[...] = jnp.zeros(a_ref.shape, F32)

    a_all = jnp.dot(h_ref[...], ut_ref[...], preferred_element_type=F32)
    i1 = i1_ref[...]
    i2 = i2_ref[...]
    acc = a_ref[...]
    for c in range(te // LANES):
        picked = jnp.take_along_axis(a_all[:, c * LANES:(c + 1) * LANES], i2, axis=1)
        acc = jnp.where(i1 == j * (te // LANES) + c, picked, acc)
    a_ref[...] = acc


def _expert_in(h2, ut, i1, i2, l):
    t, d = h2.shape
    n_exp = ut.shape[-1]
    m = i1.shape[1]
    tm, te = min(1024, t), 2048
    assert t % tm == 0 and n_exp % te == 0
    tok = lambda i, j: (i, 0)
    return pl.pallas_call(
        functools.partial(_expert_in_kernel, te=te),
        grid=(t // tm, n_exp // te),
        in_specs=[
            pl.BlockSpec((tm, d), tok),
            pl.BlockSpec((None, d, te), lambda i, j: (l, 0, j)),
            pl.BlockSpec((tm, m), tok),
            pl.BlockSpec((tm, m), tok),
        ],
        out_specs=pl.BlockSpec((tm, m), tok),
        out_shape=jax.ShapeDtypeStruct((t, m), F32),
        compiler_params=pltpu.CompilerParams(
            dimension_semantics=("arbitrary", "arbitrary"), vmem_limit_bytes=VMEM_LIMIT),
        name="expert_in",
    )(h2, ut, i1, i2)


def _expert_out_kernel(a_ref, gate_ref, i1_ref, i2_ref, v_ref, x_ref, mod_ref, fg_ref, o_ref,
                       w_scr, pbuf, lhs_scr, acc_scr, *, tm, te, n_keys, final_norm):
    j = pl.program_id(1)
    nk = te // LANES

    @pl.when(j == 0)
    def _():
        a = a_ref[...]
        w_scr[...] = gate_ref[...] * (0.5 * a * (1.0 + lax.erf(a * (2.0 ** -0.5))))
        acc_scr[...] = jnp.zeros(acc_scr.shape, F32)
        key_id = lax.broadcasted_iota(jnp.int32, (n_keys, LANES), 0)

        def place(t, carry):
            w_row = w_scr[pl.ds(t, 1), :]
            left = jnp.where(key_id == i1_ref[pl.ds(t, 1), :], w_row, 0.0).astype(BF16)
            right = jnp.where(key_id == i2_ref[pl.ds(t, 1), :], 1.0, 0.0).astype(BF16)
            pbuf[pl.ds(pl.multiple_of(t * P_PITCH, 8), n_keys), :] = lax.dot_general(
                left, right, (((1,), (1,)), ((), ())), preferred_element_type=F32)
            return carry

        lax.fori_loop(0, tm, place, 0, unroll=32)

    for c in range(nk):
        lhs_scr[:, c * LANES:(c + 1) * LANES] = pbuf[pl.ds(j * nk + c, tm, stride=P_PITCH), :].astype(BF16)
    acc_scr[...] += jnp.dot(lhs_scr[...], v_ref[...], preferred_element_type=F32)

    @pl.when(j == pl.num_programs(1) - 1)
    def _():
        y = x_ref[...] + mod_ref[5:6, :] * acc_scr[...]
        if final_norm:
            ms = jnp.mean(y * y, axis=-1, keepdims=True)
            y = y * lax.rsqrt(ms + EPS) * fg_ref[...]
        o_ref[...] = y


def _expert_out(a_sel, gate, i1, i2, vb, x2, mod, final_g, l, seq, final_norm):
    t, d = x2.shape
    n_exp = vb.shape[1]
    m = i1.shape[1]
    n_keys = LANES
    tm, te = 512, 1024
    assert t % tm == 0 and seq % tm == 0 and n_exp % te == 0 and n_exp == n_keys * n_keys
    tok = lambda i, j: (i, 0)
    kern = functools.partial(_expert_out_kernel, tm=tm, te=te, n_keys=n_keys, final_norm=final_norm)
    return pl.pallas_call(
        kern,
        grid=(t // tm, n_exp // te),
        in_specs=[
            pl.BlockSpec((tm, m), tok),
            pl.BlockSpec((tm, m), tok),
            pl.BlockSpec((tm, m), tok),
            pl.BlockSpec((tm, m), tok),
            pl.BlockSpec((None, te, d), lambda i, j: (l, j, 0)),
            pl.BlockSpec((tm, d), tok, pipeline_mode=pl.Buffered(1)),
            pl.BlockSpec((None, None, 6, d), lambda i, j: (l, (i * tm) // seq, 0, 0)),
            pl.BlockSpec((1, d), lambda i, j: (0, 0)),
        ],
        out_specs=pl.BlockSpec((tm, d), tok),
        out_shape=jax.ShapeDtypeStruct((t, d), F32),
        scratch_shapes=[
            pltpu.VMEM((tm, m), F32),
            pltpu.VMEM((tm * P_PITCH, LANES), F32),
            pltpu.VMEM((tm, te), BF16),
            pltpu.VMEM((tm, d), F32),
        ],
        compiler_params=pltpu.CompilerParams(
            dimension_semantics=("arbitrary", "arbitrary"), vmem_limit_bytes=VMEM_LIMIT),
        name="expert_out",
    )(a_sel, gate, i1, i2, vb, x2, mod, final_g)


def kernel(x, c, w_ada, b_ada, norm1_g, norm2_g, w_in, w_out, conv_w, conv_b, conv_ln_g, conv_ln_b,
           pool_w, pool_b, pool_scale, peer_wq, peer_keys, peer_u, peer_v, final_g):
    depth = w_ada.shape[0]
    b, s, d = x.shape
    row = lambda p: p.reshape(depth, 1, p.shape[-1])

    mod = _adaln(c, w_ada, b_ada).reshape(depth, b, 6, d)
    w_in_b = w_in.astype(BF16)
    w_out_b = w_out.astype(BF16)
    pool_w_b = pool_w.astype(BF16)
    wq_b = peer_wq.astype(BF16)
    keys_b = peer_keys.astype(BF16)
    ut_b = jnp.swapaxes(peer_u, 1, 2).astype(BF16)
    v_b = peer_v.astype(BF16)
    n1g, n2g = row(norm1_g), row(norm2_g)
    cb, lg, lb, ps = row(conv_b), row(conv_ln_g), row(conv_ln_b), row(pool_scale)
    fg = final_g.reshape(1, d)

    for l in range(depth):
        x = _mixer(x, mod, n1g, w_in_b, conv_w, cb, lg, lb, pool_w_b, pool_b, ps, w_out_b, l)
        x2 = x.reshape(b * s, d)
        h2, i1, i2, gate = _route(x2, mod, n2g, wq_b, keys_b, l, s)
        a_sel = _expert_in(h2, ut_b, i1, i2, l)
        x2 = _expert_out(a_sel, gate, i1, i2, v_b, x2, mod, fg, l, s, l == depth - 1)
        x = x2.reshape(b, s, d)
    return x
```

```python
import functools

import jax
import jax.numpy as jnp
from jax import lax
from jax.experimental import pallas as pl
from jax.experimental.pallas import tpu as pltpu

EPS = 1e-6
POOL_WINDOWS = (2, 4, 8, 16)
PEER_TOPK = 16
LANES = 128
HALO = 32
CONV_ROWS = 32
P_PITCH = 136
VMEM_LIMIT = 58 * 1024 * 1024

F32 = jnp.float32
BF16 = jnp.bfloat16
NEG_INF = float("-inf")


def _rms_modulate(x, gain, shift, scale):
    ms = jnp.mean(x * x, axis=-1, keepdims=True)
    return (x * lax.rsqrt(ms + EPS) * gain) * (1.0 + scale) + shift


def _adaln_kernel(c_ref, w_ref, b_ref, o_ref):
    c = c_ref[...]
    cond = c * jax.nn.sigmoid(c)
    o_ref[...] = jnp.dot(cond, w_ref[...], preferred_element_type=F32,
                         precision=lax.Precision.HIGHEST) + b_ref[...]


def _adaln(c, w_ada, b_ada):
    depth, d, n = w_ada.shape
    b = c.shape[0]
    tn = 1536
    return pl.pallas_call(
        _adaln_kernel,
        grid=(depth, n // tn),
        in_specs=[
            pl.BlockSpec((b, d), lambda l, j: (0, 0)),
            pl.BlockSpec((None, d, tn), lambda l, j: (l, 0, j)),
            pl.BlockSpec((None, 1, tn), lambda l, j: (l, 0, j)),
        ],
        out_specs=pl.BlockSpec((None, b, tn), lambda l, j: (l, 0, j)),
        out_shape=jax.ShapeDtypeStruct((depth, b, n), F32),
        compiler_params=pltpu.CompilerParams(vmem_limit_bytes=VMEM_LIMIT),
        name="adaln",
    )(c, w_ada, b_ada.reshape(depth, 1, n))


def _mixer_kernel(x_ref, mod_ref, n1g_ref, win_ref, convw_ref, convb_ref, lng_ref, lnb_ref,
                  poolw_ref, poolb_ref, pscale_ref, wout_ref, o_ref,
                  ubuf, zbuf, ya_scr, yb_scr, *, ts, d_conv, conv_width):
    s = pl.program_id(1)
    pool_group = poolw_ref.shape[-1]

    @pl.when(s == 0)
    def _():
        ubuf[0:HALO, :] = jnp.zeros((HALO, ubuf.shape[1]), F32)
        zbuf[0:HALO, :] = jnp.zeros((HALO, zbuf.shape[1]), F32)

    x = x_ref[...]
    h = _rms_modulate(x, n1g_ref[...], mod_ref[0:1, :], mod_ref[1:2, :])
    z = jnp.dot(h.astype(BF16), win_ref[...], preferred_element_type=F32)
    ubuf[HALO:HALO + ts, :] = z[:, :d_conv] * jax.nn.sigmoid(z[:, d_conv:2 * d_conv])
    zbuf[HALO:HALO + ts, :] = z[:, 2 * d_conv:]

    convb = convb_ref[...]
    lng = lng_ref[...]
    lnb = lnb_ref[...]
    for r0 in range(0, ts, CONV_ROWS):
        acc = jnp.broadcast_to(convb, (CONV_ROWS, d_conv))
        for k in range(conv_width):
            start = HALO - (conv_width - 1) + k + r0
            acc = acc + convw_ref[k:k + 1, :] * ubuf[start:start + CONV_ROWS, :]
        mu = jnp.mean(acc, axis=-1, keepdims=True)
        cen = acc - mu
        var = jnp.mean(cen * cen, axis=-1, keepdims=True)
        yn = cen * lax.rsqrt(var + EPS) * lng + lnb
        ya_scr[r0:r0 + CONV_ROWS, :] = (yn * jax.nn.sigmoid(yn)).astype(BF16)

    frame = (s * ts + lax.broadcasted_iota(jnp.int32, (ts, 1), 0) + 1).astype(F32)
    for g, w in enumerate(POOL_WINDOWS):
        lo = g * pool_group
        zg = zbuf[HALO:HALO + ts, lo:lo + pool_group]
        tot = zg
        for j in range(1, w):
            tot = tot + zbuf[HALO - j:HALO - j + ts, lo:lo + pool_group]
        p = tot / jnp.minimum(frame, float(w)) - zg
        yg = jnp.dot(p.astype(BF16), poolw_ref[g], preferred_element_type=F32) + poolb_ref[g:g + 1, :]
        yb_scr[:, lo:lo + pool_group] = (yg * pscale_ref[:, lo:lo + pool_group]).astype(BF16)

    mix = jnp.dot(ya_scr[...], wout_ref[0:d_conv, :], preferred_element_type=F32)
    mix = mix + jnp.dot(yb_scr[...], wout_ref[d_conv:, :], preferred_element_type=F32)
    o_ref[...] = x + mod_ref[2:3, :] * mix

    ubuf[0:HALO, :] = ubuf[ts:ts + HALO, :]
    zbuf[0:HALO, :] = zbuf[ts:ts + HALO, :]


def _mixer(x, mod, n1g, w_in, conv_w, conv_b, ln_g, ln_b, pool_w, pool_b, pool_scale, w_out, l):
    b, s, d = x.shape
    d_in = w_in.shape[-1]
    conv_width, d_conv = conv_w.shape[1:]
    n_groups, pool_group = pool_w.shape[1:3]
    d_pool = n_groups * pool_group
    ts = min(256, s)
    assert s % ts == 0 and ts % CONV_ROWS == 0 and ts >= HALO
    assert HALO >= conv_width - 1 and HALO >= max(POOL_WINDOWS) - 1
    assert d_in == 2 * d_conv + d_pool and n_groups == len(POOL_WINDOWS)
    lay = lambda bi, si: (l, 0, 0)
    kern = functools.partial(_mixer_kernel, ts=ts, d_conv=d_conv, conv_width=conv_width)
    return pl.pallas_call(
        kern,
        grid=(b, s // ts),
        in_specs=[
            pl.BlockSpec((None, ts, d), lambda bi, si: (bi, si, 0)),
            pl.BlockSpec((None, None, 6, d), lambda bi, si: (l, bi, 0, 0)),
            pl.BlockSpec((None, 1, d), lay),
            pl.BlockSpec((None, d, d_in), lay),
            pl.BlockSpec((None, conv_width, d_conv), lay),
            pl.BlockSpec((None, 1, d_conv), lay),
            pl.BlockSpec((None, 1, d_conv), lay),
            pl.BlockSpec((None, 1, d_conv), lay),
            pl.BlockSpec((None, n_groups, pool_group, pool_group), lambda bi, si: (l, 0, 0, 0)),
            pl.BlockSpec((None, n_groups, pool_group), lay),
            pl.BlockSpec((None, 1, d_pool), lay),
            pl.BlockSpec((None, d_conv + d_pool, d), lay),
        ],
        out_specs=pl.BlockSpec((None, ts, d), lambda bi, si: (bi, si, 0)),
        out_shape=jax.ShapeDtypeStruct((b, s, d), F32),
        scratch_shapes=[
            pltpu.VMEM((HALO + ts, d_conv), F32),
            pltpu.VMEM((HALO + ts, d_pool), F32),
            pltpu.VMEM((ts, d_conv), BF16),
            pltpu.VMEM((ts, d_pool), BF16),
        ],
        compiler_params=pltpu.CompilerParams(
            dimension_semantics=("arbitrary", "arbitrary"), vmem_limit_bytes=VMEM_LIMIT),
        name="mixer",
    )(x, mod, n1g, w_in, conv_w, conv_b, ln_g, ln_b, pool_w, pool_b, pool_scale, w_out)


def _sort_network(n):
    pairs = []
    p = 1
    while p < n:
        k = p
        while k >= 1:
            for j in range(k % p, n - k, 2 * k):
                for i in range(min(k, n - j - k)):
                    if (i + j) // (2 * p) == (i + j + k) // (2 * p):
                        pairs.append((i + j, i + j + k))
            k //= 2
        p *= 2
    return pairs


def _topk_rows(sc, k, n_keys):
    nv = n_keys // 8
    tt = sc.shape[1]
    assert k <= nv and nv & (nv - 1) == 0
    sub = lax.broadcasted_iota(jnp.int32, (8, tt), 0).astype(F32)
    val = [sc[8 * i:8 * (i + 1), :] for i in range(nv)]
    idx = [sub + float(8 * i) for i in range(nv)]
    for a, b in _sort_network(nv):
        va, vb, ia, ib = val[a], val[b], idx[a], idx[b]
        swap = (vb > va) | ((vb == va) & (ib < ia))
        val[a], val[b] = jnp.maximum(va, vb), jnp.minimum(va, vb)
        idx[a], idx[b] = jnp.where(swap, ib, ia), jnp.where(swap, ia, ib)
    top_v, top_i = [], []
    for j in range(k):
        m = jnp.max(val[0], axis=0, keepdims=True)
        win = jnp.min(jnp.where(val[0] == m, idx[0], float(n_keys)), axis=0, keepdims=True)
        top_v.append(m)
        top_i.append(win)
        popped = idx[0] == win
        for i in range(nv - 1 - j):
            val[i] = jnp.where(popped, val[i + 1], val[i])
            idx[i] = jnp.where(popped, idx[i + 1], idx[i])
    return top_v, top_i


def _route_kernel(x_ref, mod_ref, n2g_ref, wq_ref, keys_ref, h_ref, i1_ref, i2_ref, gate_ref,
                  q_scr, tv_scr, ti_scr, a_scr, b_scr, e_scr, g_scr, *, tt, heads, n_keys):
    k = PEER_TOPK
    x = x_ref[...]
    h = _rms_modulate(x, n2g_ref[...], mod_ref[3:4, :], mod_ref[4:5, :]).astype(BF16)
    h_ref[...] = h
    q = jnp.dot(h, wq_ref[...], preferred_element_type=F32).astype(BF16)
    half = keys_ref.shape[-1]
    for hp in range(2 * heads):
        q_scr[hp] = q[:, hp * half:(hp + 1) * half]

    def stage1(hd, carry):
        for p in range(2):
            hp = 2 * hd + p
            sc = lax.dot_general(keys_ref[hp], q_scr[hp], (((1,), (1,)), ((), ())),
                                 preferred_element_type=F32)
            top_v, top_i = _topk_rows(sc, k, n_keys)
            for j in range(k):
                tv_scr[hp, j:j + 1, :] = top_v[j]
                ti_scr[hp, j:j + 1, :] = top_i[j]
        return carry

    lax.fori_loop(0, heads, stage1, 0)

    hk = k // 2
    row_lo = lax.broadcasted_iota(jnp.int32, (hk, tt), 0).astype(F32)
    row_hi = row_lo + float(hk)

    def stage2(hd, carry):
        base = pl.multiple_of(hd * k, k)
        tv1 = tv_scr[2 * hd]
        tv2 = tv_scr[2 * hd + 1]
        head_lo = tv1[0:hk, :] + tv2[0:1, :]
        head_hi = tv1[hk:, :] + tv2[0:1, :]
        rest = [tv1[0:hk, :] + tv2[b:b + 1, :] for b in range(1, k)]
        ptr = jnp.zeros((hk, tt), F32)
        best = []
        for j in range(k):
            m = jnp.maximum(jnp.max(head_lo, axis=0, keepdims=True),
                            jnp.max(head_hi, axis=0, keepdims=True))
            a = jnp.minimum(
                jnp.min(jnp.where(head_lo == m, row_lo, float(k)), axis=0, keepdims=True),
                jnp.min(jnp.where(head_hi == m, row_hi, float(k)), axis=0, keepdims=True))
            pop_lo = row_lo == a
            a_scr[pl.ds(base + j, 1), :] = a
            b_scr[pl.ds(base + j, 1), :] = jnp.max(jnp.where(pop_lo, ptr, 0.0), axis=0, keepdims=True)
            best.append(m)
            ptr = jnp.where(pop_lo, ptr + 1.0, ptr)
            head_hi = jnp.where(row_hi == a, NEG_INF, head_hi)
            if j < k - 1:
                head_lo = jnp.where(pop_lo, rest[0], head_lo)
            for i in range(k - 2 - j):
                rest[i] = jnp.where(pop_lo, rest[i + 1], rest[i])
        ex = [jnp.exp(m - best[0]) for m in best]
        den = ex[0]
        for j in range(1, k):
            den = den + ex[j]
        for j in range(k):
            g_scr[pl.ds(base + j, 1), :] = ex[j] / den
        sel_a = a_scr[pl.ds(base, k), :]
        sel_b = b_scr[pl.ds(base, k), :]
        key1 = jnp.zeros((k, tt), F32)
        key2 = jnp.zeros((k, tt), F32)
        for r in range(k):
            key1 = jnp.where(sel_a == float(r), ti_scr[2 * hd, r:r + 1, :], key1)
            key2 = jnp.where(sel_b == float(r), ti_scr[2 * hd + 1, r:r + 1, :], key2)
        e_scr[pl.ds(base, k), :] = key1 * float(n_keys) + key2
        return carry

    lax.fori_loop(0, heads, stage2, 0)

    e = e_scr[...].T.astype(jnp.int32)
    key_bits = n_keys.bit_length() - 1
    i1_ref[...] = lax.shift_right_logical(e, key_bits)
    i2_ref[...] = lax.bitwise_and(e, n_keys - 1)
    gate_ref[...] = g_scr[...].T


def _route(x2, mod, n2g, wq, keys, l, seq):
    t, d = x2.shape
    heads, _, n_keys, half = keys.shape[1:]
    m = heads * PEER_TOPK
    tt = 256
    assert seq % tt == 0 and m == LANES and n_keys == LANES and n_keys & (n_keys - 1) == 0
    keys2 = keys.reshape(keys.shape[0], 2 * heads, n_keys, half)
    kern = functools.partial(_route_kernel, tt=tt, heads=heads, n_keys=n_keys)
    tok = lambda i: (i, 0)
    return pl.pallas_call(
        kern,
        grid=(t // tt,),
        in_specs=[
            pl.BlockSpec((tt, d), tok),
            pl.BlockSpec((None, None, 6, d), lambda i: (l, (i * tt) // seq, 0, 0)),
            pl.BlockSpec((None, 1, d), lambda i: (l, 0, 0)),
            pl.BlockSpec((None, d, 2 * heads * half), lambda i: (l, 0, 0)),
            pl.BlockSpec((None, 2 * heads, n_keys, half), lambda i: (l, 0, 0, 0)),
        ],
        out_specs=[pl.BlockSpec((tt, d), tok), pl.BlockSpec((tt, m), tok),
                   pl.BlockSpec((tt, m), tok), pl.BlockSpec((tt, m), tok)],
        out_shape=[jax.ShapeDtypeStruct((t, d), BF16), jax.ShapeDtypeStruct((t, m), jnp.int32),
                   jax.ShapeDtypeStruct((t, m), jnp.int32), jax.ShapeDtypeStruct((t, m), F32)],
        scratch_shapes=[
            pltpu.VMEM((2 * heads, tt, half), BF16),
            pltpu.VMEM((2 * heads, PEER_TOPK, tt), F32),
            pltpu.VMEM((2 * heads, PEER_TOPK, tt), F32),
            pltpu.VMEM((m, tt), F32),
            pltpu.VMEM((m, tt), F32),
            pltpu.VMEM((m, tt), F32),
            pltpu.VMEM((m, tt), F32),
        ],
        compiler_params=pltpu.CompilerParams(
            dimension_semantics=("arbitrary",), vmem_limit_bytes=VMEM_LIMIT),
        name="route",
    )(x2, mod, n2g, wq, keys2)


def _expert_in_kernel(h_ref, ut_ref, i1_ref, i2_ref, a_ref, *, te):
    j = pl.program_id(1)

    @pl.when(j == 0)
    def _():
        a_ref[...] = jnp.zeros(a_ref.shape, F32)

    a_all = jnp.dot(h_ref[...], ut_ref[...], preferred_element_type=F32)
    i1 = i1_ref[...]
    i2 = i2_ref[...]
    acc = a_ref[...]
    for c in range(te // LANES):
        picked = jnp.take_along_axis(a_all[:, c * LANES:(c + 1) * LANES], i2, axis=1)
        acc = jnp.where(i1 == j * (te // LANES) + c, picked, acc)
    a_ref[...] = acc


def _expert_in(h2, ut, i1, i2, l):
    t, d = h2.shape
    n_exp = ut.shape[-1]
    m = i1.shape[1]
    tm, te = min(1024, t), 2048
    assert t % tm == 0 and n_exp % te == 0
    tok = lambda i, j: (i, 0)
    return pl.pallas_call(
        functools.partial(_expert_in_kernel, te=te),
        grid=(t // tm, n_exp // te),
        in_specs=[
            pl.BlockSpec((tm, d), tok),
            pl.BlockSpec((None, d, te), lambda i, j: (l, 0, j)),
            pl.BlockSpec((tm, m), tok),
            pl.BlockSpec((tm, m), tok),
        ],
        out_specs=pl.BlockSpec((tm, m), tok),
        out_shape=jax.ShapeDtypeStruct((t, m), F32),
        compiler_params=pltpu.CompilerParams(
            dimension_semantics=("arbitrary", "arbitrary"), vmem_limit_bytes=VMEM_LIMIT),
        name="expert_in",
    )(h2, ut, i1, i2)


def _expert_out_kernel(a_ref, gate_ref, i1_ref, i2_ref, v_ref, x_ref, mod_ref, fg_ref, o_ref,
                       w_scr, pbuf, lhs_scr, *, tm, te, n_keys, final_norm):
    j = pl.program_id(1)
    nk = te // LANES

    @pl.when(j == 0)
    def _():
        a = a_ref[...]
        w_scr[...] = gate_ref[...] * (0.5 * a * (1.0 + lax.erf(a * (2.0 ** -0.5))))
        o_ref[...] = jnp.zeros(o_ref.shape, F32)
        key_id = lax.broadcasted_iota(jnp.int32, (n_keys, LANES), 0)

        def place(t, carry):
            w_row = w_scr[pl.ds(t, 1), :]
            left = jnp.where(key_id == i1_ref[pl.ds(t, 1), :], w_row, 0.0).astype(BF16)
            right = jnp.where(key_id == i2_ref[pl.ds(t, 1), :], 1.0, 0.0).astype(BF16)
            pbuf[pl.ds(pl.multiple_of(t * P_PITCH, 8), n_keys), :] = lax.dot_general(
                left, right, (((1,), (1,)), ((), ())), preferred_element_type=F32)
            return carry

        lax.fori_loop(0, tm, place, 0, unroll=32)

    for c in range(nk):
        lhs_scr[:, c * LANES:(c + 1) * LANES] = pbuf[pl.ds(j * nk + c, tm, stride=P_PITCH), :].astype(BF16)
    o_ref[...] += jnp.dot(lhs_scr[...], v_ref[...], preferred_element_type=F32)

    @pl.when(j == pl.num_programs(1) - 1)
    def _():
        y = x_ref[...] + mod_ref[5:6, :] * o_ref[...]
        if final_norm:
            ms = jnp.mean(y * y, axis=-1, keepdims=True)
            y = y * lax.rsqrt(ms + EPS) * fg_ref[...]
        o_ref[...] = y


def _expert_out(a_sel, gate, i1, i2, vb, x2, mod, final_g, l, seq, final_norm):
    t, d = x2.shape
    n_exp = vb.shape[1]
    m = i1.shape[1]
    n_keys = LANES
    tm, te = 512, 2048
    assert t % tm == 0 and seq % tm == 0 and n_exp % te == 0 and n_exp == n_keys * n_keys
    tok = lambda i, j: (i, 0)
    kern = functools.partial(_expert_out_kernel, tm=tm, te=te, n_keys=n_keys, final_norm=final_norm)
    return pl.pallas_call(
        kern,
        grid=(t // tm, n_exp // te),
        in_specs=[
            pl.BlockSpec((tm, m), tok),
            pl.BlockSpec((tm, m), tok),
            pl.BlockSpec((tm, m), tok),
            pl.BlockSpec((tm, m), tok),
            pl.BlockSpec((None, te, d), lambda i, j: (l, j, 0)),
            pl.BlockSpec((tm, d), tok, pipeline_mode=pl.Buffered(1)),
            pl.BlockSpec((None, None, 6, d), lambda i, j: (l, (i * tm) // seq, 0, 0)),
            pl.BlockSpec((1, d), lambda i, j: (0, 0)),
        ],
        out_specs=pl.BlockSpec((tm, d), tok),
        out_shape=jax.ShapeDtypeStruct((t, d), F32),
        scratch_shapes=[
            pltpu.VMEM((tm, m), F32),
            pltpu.VMEM((tm * P_PITCH, LANES), F32),
            pltpu.VMEM((tm, te), BF16),
        ],
        compiler_params=pltpu.CompilerParams(
            dimension_semantics=("arbitrary", "arbitrary"), vmem_limit_bytes=VMEM_LIMIT),
        name="expert_out",
    )(a_sel, gate, i1, i2, vb, x2, mod, final_g)


def kernel(x, c, w_ada, b_ada, norm1_g, norm2_g, w_in, w_out, conv_w, conv_b, conv_ln_g, conv_ln_b,
           pool_w, pool_b, pool_scale, peer_wq, peer_keys, peer_u, peer_v, final_g):
    depth = w_ada.shape[0]
    b, s, d = x.shape
    row = lambda p: p.reshape(depth, 1, p.shape[-1])

    mod = _adaln(c, w_ada, b_ada).reshape(depth, b, 6, d)
    w_in_b = w_in.astype(BF16)
    w_out_b = w_out.astype(BF16)
    pool_w_b = pool_w.astype(BF16)
    wq_b = peer_wq.astype(BF16)
    keys_b = peer_keys.astype(BF16)
    ut_b = jnp.swapaxes(peer_u, 1, 2).astype(BF16)
    v_b = peer_v.astype(BF16)
    n1g, n2g = row(norm1_g), row(norm2_g)
    cb, lg, lb, ps = row(conv_b), row(conv_ln_g), row(conv_ln_b), row(pool_scale)
    fg = final_g.reshape(1, d)

    for l in range(depth):
        x = _mixer(x, mod, n1g, w_in_b, conv_w, cb, lg, lb, pool_w_b, pool_b, ps, w_out_b, l)
        x2 = x.reshape(b * s, d)
        h2, i1, i2, gate = _route(x2, mod, n2g, wq_b, keys_b, l, s)
        a_sel = _expert_in(h2, ut_b, i1, i2, l)
        x2 = _expert_out(a_sel, gate, i1, i2, v_b, x2, mod, fg, l, s, l == depth - 1)
        x = x2.reshape(b, s, d)
    return x
```

```python
import functools

import jax
import jax.numpy as jnp
from jax import lax
from jax.experimental import pallas as pl
from jax.experimental.pallas import tpu as pltpu

EPS = 1e-6
POOL_WINDOWS = (2, 4, 8, 16)
PEER_TOPK = 16
LANES = 128
SUBLANES = 8
MXU_COLS = 256
HALO = 32
CONV_ROWS = 32
P_PITCH = 136
VMEM_LIMIT = 58 * 1024 * 1024

F32 = jnp.float32
BF16 = jnp.bfloat16
NEG_INF = float("-inf")


def _rms_modulate(x, gain, shift, scale):
    ms = jnp.mean(x * x, axis=-1, keepdims=True)
    return (x * lax.rsqrt(ms + EPS) * gain) * (1.0 + scale) + shift


def _adaln_kernel(c_ref, w_ref, b_ref, o_ref):
    c = c_ref[...]
    cond = c * jax.nn.sigmoid(c)
    o_ref[...] = jnp.dot(cond, w_ref[...], preferred_element_type=F32,
                         precision=lax.Precision.HIGHEST) + b_ref[...]


def _adaln(c, w_ada, b_ada):
    depth, d, n = w_ada.shape
    b = c.shape[0]
    tn = 1536
    return pl.pallas_call(
        _adaln_kernel,
        grid=(depth, n // tn),
        in_specs=[
            pl.BlockSpec((b, d), lambda l, j: (0, 0)),
            pl.BlockSpec((None, d, tn), lambda l, j: (l, 0, j)),
            pl.BlockSpec((None, 1, tn), lambda l, j: (l, 0, j)),
        ],
        out_specs=pl.BlockSpec((None, b, tn), lambda l, j: (l, 0, j)),
        out_shape=jax.ShapeDtypeStruct((depth, b, n), F32),
        compiler_params=pltpu.CompilerParams(vmem_limit_bytes=VMEM_LIMIT),
        name="adaln",
    )(c, w_ada, b_ada.reshape(depth, 1, n))


def _mixer_kernel(x_ref, mod_ref, n1g_ref, win_ref, convw_ref, convb_ref, lng_ref, lnb_ref,
                  poolw_ref, poolb_ref, pscale_ref, wout_ref, o_ref,
                  ubuf, zbuf, ya_scr, yb_scr, *, ts, d_conv, conv_width):
    s = pl.program_id(1)
    pool_group = poolw_ref.shape[-1]

    @pl.when(s == 0)
    def _():
        ubuf[0:HALO, :] = jnp.zeros((HALO, ubuf.shape[1]), F32)
        zbuf[0:HALO, :] = jnp.zeros((HALO, zbuf.shape[1]), F32)

    x = x_ref[...]
    h = _rms_modulate(x, n1g_ref[...], mod_ref[0:1, :], mod_ref[1:2, :])
    z = jnp.dot(h.astype(BF16), win_ref[...], preferred_element_type=F32)
    ubuf[HALO:HALO + ts, :] = z[:, :d_conv] * jax.nn.sigmoid(z[:, d_conv:2 * d_conv])
    zbuf[HALO:HALO + ts, :] = z[:, 2 * d_conv:]

    convb = convb_ref[...]
    lng = lng_ref[...]
    lnb = lnb_ref[...]
    for r0 in range(0, ts, CONV_ROWS):
        acc = jnp.broadcast_to(convb, (CONV_ROWS, d_conv))
        for rho in range(SUBLANES):
            part = None
            for q in range((conv_width - 1 - rho) // SUBLANES + 1):
                lag = SUBLANES * q + rho
                base = HALO + r0 - SUBLANES - SUBLANES * q
                term = (convw_ref[conv_width - 1 - lag:conv_width - lag, :]
                        * ubuf[base:base + CONV_ROWS + SUBLANES, :])
                part = term if part is None else part + term
            acc = acc + part[SUBLANES - rho:SUBLANES - rho + CONV_ROWS, :]
        mu = jnp.mean(acc, axis=-1, keepdims=True)
        cen = acc - mu
        var = jnp.mean(cen * cen, axis=-1, keepdims=True)
        yn = cen * lax.rsqrt(var + EPS) * lng + lnb
        ya_scr[r0:r0 + CONV_ROWS, :] = (yn * jax.nn.sigmoid(yn)).astype(BF16)

    frame = (s * ts + lax.broadcasted_iota(jnp.int32, (ts, 1), 0) + 1).astype(F32)
    for g, w in enumerate(POOL_WINDOWS):
        lo = g * pool_group
        zg = zbuf[HALO:HALO + ts, lo:lo + pool_group]
        tot = zg
        for j in range(1, w):
            tot = tot + zbuf[HALO - j:HALO - j + ts, lo:lo + pool_group]
        p = tot / jnp.minimum(frame, float(w)) - zg
        yg = jnp.dot(p.astype(BF16), poolw_ref[g], preferred_element_type=F32) + poolb_ref[g:g + 1, :]
        yb_scr[:, lo:lo + pool_group] = (yg * pscale_ref[:, lo:lo + pool_group]).astype(BF16)

    mix = jnp.dot(ya_scr[...], wout_ref[0:d_conv, :], preferred_element_type=F32)
    mix = mix + jnp.dot(yb_scr[...], wout_ref[d_conv:, :], preferred_element_type=F32)
    o_ref[...] = x + mod_ref[2:3, :] * mix

    ubuf[0:HALO, :] = ubuf[ts:ts + HALO, :]
    zbuf[0:HALO, :] = zbuf[ts:ts + HALO, :]


def _mixer(x, mod, n1g, w_in, conv_w, conv_b, ln_g, ln_b, pool_w, pool_b, pool_scale, w_out, l):
    b, s, d = x.shape
    d_in = w_in.shape[-1]
    conv_width, d_conv = conv_w.shape[1:]
    n_groups, pool_group = pool_w.shape[1:3]
    d_pool = n_groups * pool_group
    ts = min(256, s)
    assert s % ts == 0 and ts % CONV_ROWS == 0 and ts >= HALO
    assert HALO >= SUBLANES * ((conv_width - 1) // SUBLANES + 1) and HALO >= max(POOL_WINDOWS) - 1
    assert d_in == 2 * d_conv + d_pool and n_groups == len(POOL_WINDOWS)
    lay = lambda bi, si: (l, 0, 0)
    kern = functools.partial(_mixer_kernel, ts=ts, d_conv=d_conv, conv_width=conv_width)
    return pl.pallas_call(
        kern,
        grid=(b, s // ts),
        in_specs=[
            pl.BlockSpec((None, ts, d), lambda bi, si: (bi, si, 0)),
            pl.BlockSpec((None, None, 6, d), lambda bi, si: (l, bi, 0, 0)),
            pl.BlockSpec((None, 1, d), lay),
            pl.BlockSpec((None, d, d_in), lay),
            pl.BlockSpec((None, conv_width, d_conv), lay),
            pl.BlockSpec((None, 1, d_conv), lay),
            pl.BlockSpec((None, 1, d_conv), lay),
            pl.BlockSpec((None, 1, d_conv), lay),
            pl.BlockSpec((None, n_groups, pool_group, pool_group), lambda bi, si: (l, 0, 0, 0)),
            pl.BlockSpec((None, n_groups, pool_group), lay),
            pl.BlockSpec((None, 1, d_pool), lay),
            pl.BlockSpec((None, d_conv + d_pool, d), lay),
        ],
        out_specs=pl.BlockSpec((None, ts, d), lambda bi, si: (bi, si, 0)),
        out_shape=jax.ShapeDtypeStruct((b, s, d), F32),
        scratch_shapes=[
            pltpu.VMEM((HALO + ts, d_conv), F32),
            pltpu.VMEM((HALO + ts, d_pool), F32),
            pltpu.VMEM((ts, d_conv), BF16),
            pltpu.VMEM((ts, d_pool), BF16),
        ],
        compiler_params=pltpu.CompilerParams(
            dimension_semantics=("arbitrary", "arbitrary"), vmem_limit_bytes=VMEM_LIMIT),
        name="mixer",
    )(x, mod, n1g, w_in, conv_w, conv_b, ln_g, ln_b, pool_w, pool_b, pool_scale, w_out)


def _sort_network(n):
    pairs = []
    p = 1
    while p < n:
        k = p
        while k >= 1:
            for j in range(k % p, n - k, 2 * k):
                for i in range(min(k, n - j - k)):
                    if (i + j) // (2 * p) == (i + j + k) // (2 * p):
                        pairs.append((i + j, i + j + k))
            k //= 2
        p *= 2
    return pairs


def _topk_rows(sc, k, n_keys):
    nv = n_keys // 8
    tt = sc.shape[1]
    assert k <= nv and nv & (nv - 1) == 0
    sub = lax.broadcasted_iota(jnp.int32, (8, tt), 0).astype(F32)
    val = [sc[8 * i:8 * (i + 1), :] for i in range(nv)]
    idx = [sub + float(8 * i) for i in range(nv)]
    for a, b in _sort_network(nv):
        va, vb, ia, ib = val[a], val[b], idx[a], idx[b]
        swap = (vb > va) | ((vb == va) & (ib < ia))
        val[a], val[b] = jnp.maximum(va, vb), jnp.minimum(va, vb)
        idx[a], idx[b] = jnp.where(swap, ib, ia), jnp.where(swap, ia, ib)
    top_v, top_i = [], []
    for j in range(k):
        m = jnp.max(val[0], axis=0, keepdims=True)
        win = jnp.min(jnp.where(val[0] == m, idx[0], float(n_keys)), axis=0, keepdims=True)
        top_v.append(m)
        top_i.append(win)
        popped = idx[0] == win
        for i in range(nv - 1 - j):
            val[i] = jnp.where(popped, val[i + 1], val[i])
            idx[i] = jnp.where(popped, idx[i + 1], idx[i])
    return top_v, top_i


def _route_kernel(x_ref, mod_ref, n2g_ref, wq_ref, keys_ref, h_ref, i1_ref, i2_ref, gate_ref,
                  q_scr, tv_scr, ti_scr, a_scr, b_scr, e_scr, g_scr, *, tt, heads, n_keys):
    k = PEER_TOPK
    x = x_ref[...]
    h = _rms_modulate(x, n2g_ref[...], mod_ref[3:4, :], mod_ref[4:5, :]).astype(BF16)
    h_ref[...] = h
    q = jnp.dot(h, wq_ref[...], preferred_element_type=F32).astype(BF16)
    half = keys_ref.shape[-1]
    for hp in range(2 * heads):
        q_scr[hp] = q[:, hp * half:(hp + 1) * half]

    def stage1(hd, carry):
        for p in range(2):
            hp = 2 * hd + p
            sc = lax.dot_general(keys_ref[hp], q_scr[hp], (((1,), (1,)), ((), ())),
                                 preferred_element_type=F32)
            top_v, top_i = _topk_rows(sc, k, n_keys)
            for j in range(k):
                tv_scr[hp, j:j + 1, :] = top_v[j]
                ti_scr[hp, j:j + 1, :] = top_i[j]
        return carry

    lax.fori_loop(0, heads, stage1, 0)

    hk = k // 2
    row_lo = lax.broadcasted_iota(jnp.int32, (hk, tt), 0).astype(F32)
    row_hi = row_lo + float(hk)

    def stage2(hd, carry):
        base = pl.multiple_of(hd * k, k)
        tv1 = tv_scr[2 * hd]
        tv2 = tv_scr[2 * hd + 1]
        head_lo = tv1[0:hk, :] + tv2[0:1, :]
        head_hi = tv1[hk:, :] + tv2[0:1, :]
        rest = [tv1[0:hk, :] + tv2[b:b + 1, :] for b in range(1, k)]
        ptr = jnp.zeros((hk, tt), F32)
        best = []
        for j in range(k):
            m = jnp.maximum(jnp.max(head_lo, axis=0, keepdims=True),
                            jnp.max(head_hi, axis=0, keepdims=True))
            a = jnp.minimum(
                jnp.min(jnp.where(head_lo == m, row_lo, float(k)), axis=0, keepdims=True),
                jnp.min(jnp.where(head_hi == m, row_hi, float(k)), axis=0, keepdims=True))
            pop_lo = row_lo == a
            a_scr[pl.ds(base + j, 1), :] = a
            b_scr[pl.ds(base + j, 1), :] = jnp.max(jnp.where(pop_lo, ptr, 0.0), axis=0, keepdims=True)
            best.append(m)
            ptr = jnp.where(pop_lo, ptr + 1.0, ptr)
            head_hi = jnp.where(row_hi == a, NEG_INF, head_hi)
            if j < k - 1:
                head_lo = jnp.where(pop_lo, rest[0], head_lo)
            for i in range(k - 2 - j):
                rest[i] = jnp.where(pop_lo, rest[i + 1], rest[i])
        ex = [jnp.exp(m - best[0]) for m in best]
        den = ex[0]
        for j in range(1, k):
            den = den + ex[j]
        for j in range(k):
            g_scr[pl.ds(base + j, 1), :] = ex[j] / den
        sel_a = a_scr[pl.ds(base, k), :]
        sel_b = b_scr[pl.ds(base, k), :]
        key1 = jnp.zeros((k, tt), F32)
        key2 = jnp.zeros((k, tt), F32)
        for r in range(k):
            key1 = jnp.where(sel_a == float(r), ti_scr[2 * hd, r:r + 1, :], key1)
            key2 = jnp.where(sel_b == float(r), ti_scr[2 * hd + 1, r:r + 1, :], key2)
        e_scr[pl.ds(base, k), :] = key1 * float(n_keys) + key2
        return carry

    lax.fori_loop(0, heads, stage2, 0)

    e = e_scr[...].T.astype(jnp.int32)
    key_bits = n_keys.bit_length() - 1
    i1_ref[...] = lax.shift_right_logical(e, key_bits)
    i2_ref[...] = lax.bitwise_and(e, n_keys - 1)
    gate_ref[...] = g_scr[...].T


def _route(x2, mod, n2g, wq, keys, l, seq):
    t, d = x2.shape
    heads, _, n_keys, half = keys.shape[1:]
    m = heads * PEER_TOPK
    tt = 256
    assert seq % tt == 0 and m == LANES and n_keys == LANES and n_keys & (n_keys - 1) == 0
    keys2 = keys.reshape(keys.shape[0], 2 * heads, n_keys, half)
    kern = functools.partial(_route_kernel, tt=tt, heads=heads, n_keys=n_keys)
    tok = lambda i: (i, 0)
    return pl.pallas_call(
        kern,
        grid=(t // tt,),
        in_specs=[
            pl.BlockSpec((tt, d), tok),
            pl.BlockSpec((None, None, 6, d), lambda i: (l, (i * tt) // seq, 0, 0)),
            pl.BlockSpec((None, 1, d), lambda i: (l, 0, 0)),
            pl.BlockSpec((None, d, 2 * heads * half), lambda i: (l, 0, 0)),
            pl.BlockSpec((None, 2 * heads, n_keys, half), lambda i: (l, 0, 0, 0)),
        ],
        out_specs=[pl.BlockSpec((tt, d), tok), pl.BlockSpec((tt, m), tok),
                   pl.BlockSpec((tt, m), tok), pl.BlockSpec((tt, m), tok)],
        out_shape=[jax.ShapeDtypeStruct((t, d), BF16), jax.ShapeDtypeStruct((t, m), jnp.int32),
                   jax.ShapeDtypeStruct((t, m), jnp.int32), jax.ShapeDtypeStruct((t, m), F32)],
        scratch_shapes=[
            pltpu.VMEM((2 * heads, tt, half), BF16),
            pltpu.VMEM((2 * heads, PEER_TOPK, tt), F32),
            pltpu.VMEM((2 * heads, PEER_TOPK, tt), F32),
            pltpu.VMEM((m, tt), F32),
            pltpu.VMEM((m, tt), F32),
            pltpu.VMEM((m, tt), F32),
            pltpu.VMEM((m, tt), F32),
        ],
        compiler_params=pltpu.CompilerParams(
            dimension_semantics=("arbitrary",), vmem_limit_bytes=VMEM_LIMIT),
        name="route",
    )(x2, mod, n2g, wq, keys2)


def _expert_in_kernel(h_ref, u_ref, i1_ref, i2_ref, a_ref, *, te):
    j = pl.program_id(1)

    @pl.when(j == 0)
    def _():
        a_ref[...] = jnp.zeros(a_ref.shape, F32)

    h = h_ref[...]
    i1 = i1_ref[...]
    i2 = i2_ref[...]
    acc = a_ref[...]
    for cb in range(te // MXU_COLS):
        a_blk = lax.dot_general(h, u_ref[cb * MXU_COLS:(cb + 1) * MXU_COLS, :], (((1,), (1,)), ((), ())),
                                preferred_element_type=F32)
        for cc in range(MXU_COLS // LANES):
            c = cb * (MXU_COLS // LANES) + cc
            picked = jnp.take_along_axis(a_blk[:, cc * LANES:(cc + 1) * LANES], i2, axis=1)
            acc = jnp.where(i1 == j * (te // LANES) + c, picked, acc)
    a_ref[...] = acc


def _expert_in(h2, ub, i1, i2, l):
    t, d = h2.shape
    n_exp = ub.shape[1]
    m = i1.shape[1]
    tm, te = min(1024, t), 2048
    assert t % tm == 0 and n_exp % te == 0
    tok = lambda i, j: (i, 0)
    return pl.pallas_call(
        functools.partial(_expert_in_kernel, te=te),
        grid=(t // tm, n_exp // te),
        in_specs=[
            pl.BlockSpec((tm, d), tok),
            pl.BlockSpec((None, te, d), lambda i, j: (l, j, 0)),
            pl.BlockSpec((tm, m), tok),
            pl.BlockSpec((tm, m), tok),
        ],
        out_specs=pl.BlockSpec((tm, m), tok),
        out_shape=jax.ShapeDtypeStruct((t, m), F32),
        compiler_params=pltpu.CompilerParams(
            dimension_semantics=("arbitrary", "arbitrary"), vmem_limit_bytes=VMEM_LIMIT),
        name="expert_in",
    )(h2, ub, i1, i2)


def _expert_out_kernel(a_ref, gate_ref, i1_ref, i2_ref, v_ref, x_ref, mod_ref, fg_ref, o_ref,
                       w_scr, pbuf, lhs_scr, *, tm, te, n_keys, final_norm):
    j = pl.program_id(1)
    nk = te // LANES

    @pl.when(j == 0)
    def _():
        a = a_ref[...]
        w_scr[...] = gate_ref[...] * (0.5 * a * (1.0 + lax.erf(a * (2.0 ** -0.5))))
        o_ref[...] = jnp.zeros(o_ref.shape, F32)
        key_id = lax.broadcasted_iota(jnp.int32, (n_keys, LANES), 0)

        def place(t, carry):
            w_row = w_scr[pl.ds(t, 1), :]
            left = jnp.where(key_id == i1_ref[pl.ds(t, 1), :], w_row, 0.0).astype(BF16)
            right = jnp.where(key_id == i2_ref[pl.ds(t, 1), :], 1.0, 0.0).astype(BF16)
            pbuf[pl.ds(pl.multiple_of(t * P_PITCH, 8), n_keys), :] = lax.dot_general(
                left, right, (((1,), (1,)), ((), ())), preferred_element_type=F32)
            return carry

        lax.fori_loop(0, tm, place, 0, unroll=32)

    for c in range(nk):
        lhs_scr[:, c * LANES:(c + 1) * LANES] = pbuf[pl.ds(j * nk + c, tm, stride=P_PITCH), :].astype(BF16)
    o_ref[...] += jnp.dot(lhs_scr[...], v_ref[...], preferred_element_type=F32)

    @pl.when(j == pl.num_programs(1) - 1)
    def _():
        y = x_ref[...] + mod_ref[5:6, :] * o_ref[...]
        if final_norm:
            ms = jnp.mean(y * y, axis=-1, keepdims=True)
            y = y * lax.rsqrt(ms + EPS) * fg_ref[...]
        o_ref[...] = y


def _expert_out(a_sel, gate, i1, i2, vb, x2, mod, final_g, l, seq, final_norm):
    t, d = x2.shape
    n_exp = vb.shape[1]
    m = i1.shape[1]
    n_keys = LANES
    tm, te = 512, 2048
    assert t % tm == 0 and seq % tm == 0 and n_exp % te == 0 and n_exp == n_keys * n_keys
    tok = lambda i, j: (i, 0)
    kern = functools.partial(_expert_out_kernel, tm=tm, te=te, n_keys=n_keys, final_norm=final_norm)
    return pl.pallas_call(
        kern,
        grid=(t // tm, n_exp // te),
        in_specs=[
            pl.BlockSpec((tm, m), tok),
            pl.BlockSpec((tm, m), tok),
            pl.BlockSpec((tm, m), tok),
            pl.BlockSpec((tm, m), tok),
            pl.BlockSpec((None, te, d), lambda i, j: (l, j, 0)),
            pl.BlockSpec((tm, d), tok, pipeline_mode=pl.Buffered(1)),
            pl.BlockSpec((None, None, 6, d), lambda i, j: (l, (i * tm) // seq, 0, 0)),
            pl.BlockSpec((1, d), lambda i, j: (0, 0)),
        ],
        out_specs=pl.BlockSpec((tm, d), tok),
        out_shape=jax.ShapeDtypeStruct((t, d), F32),
        scratch_shapes=[
            pltpu.VMEM((tm, m), F32),
            pltpu.VMEM((tm * P_PITCH, LANES), F32),
            pltpu.VMEM((tm, te), BF16),
        ],
        compiler_params=pltpu.CompilerParams(
            dimension_semantics=("arbitrary", "arbitrary"), vmem_limit_bytes=VMEM_LIMIT),
        name="expert_out",
    )(a_sel, gate, i1, i2, vb, x2, mod, final_g)


def kernel(x, c, w_ada, b_ada, norm1_g, norm2_g, w_in, w_out, conv_w, conv_b, conv_ln_g, conv_ln_b,
           pool_w, pool_b, pool_scale, peer_wq, peer_keys, peer_u, peer_v, final_g):
    depth = w_ada.shape[0]
    b, s, d = x.shape
    row = lambda p: p.reshape(depth, 1, p.shape[-1])

    mod = _adaln(c, w_ada, b_ada).reshape(depth, b, 6, d)
    w_in_b = w_in.astype(BF16)
    w_out_b = w_out.astype(BF16)
    pool_w_b = pool_w.astype(BF16)
    wq_b = peer_wq.astype(BF16)
    keys_b = peer_keys.astype(BF16)
    u_b = peer_u.astype(BF16)
    v_b = peer_v.astype(BF16)
    n1g, n2g = row(norm1_g), row(norm2_g)
    cb, lg, lb, ps = row(conv_b), row(conv_ln_g), row(conv_ln_b), row(pool_scale)
    fg = final_g.reshape(1, d)

    for l in range(depth):
        x = _mixer(x, mod, n1g, w_in_b, conv_w, cb, lg, lb, pool_w_b, pool_b, ps, w_out_b, l)
        x2 = x.reshape(b * s, d)
        h2, i1, i2, gate = _route(x2, mod, n2g, wq_b, keys_b, l, s)
        a_sel = _expert_in(h2, u_b, i1, i2, l)
        x2 = _expert_out(a_sel, gate, i1, i2, v_b, x2, mod, fg, l, s, l == depth - 1)
        x = x2.reshape(b, s, d)
    return x
```

```python
import functools

import jax
import jax.numpy as jnp
from jax import lax
from jax.experimental import pallas as pl
from jax.experimental.pallas import tpu as pltpu

EPS = 1e-6
POOL_WINDOWS = (2, 4, 8, 16)
PEER_TOPK = 16
LANES = 128
SUBLANES = 8
MXU_COLS = 256
HALO = 32
CONV_ROWS = 32
P_PITCH = 136
VMEM_LIMIT = 58 * 1024 * 1024

F32 = jnp.float32
BF16 = jnp.bfloat16
NEG_INF = float("-inf")


def _rms_modulate(x, gain, shift, scale):
    ms = jnp.mean(x * x, axis=-1, keepdims=True)
    return (x * lax.rsqrt(ms + EPS) * gain) * (1.0 + scale) + shift


def _adaln_kernel(c_ref, w_ref, b_ref, o_ref):
    c = c_ref[...]
    cond = c * jax.nn.sigmoid(c)
    o_ref[...] = jnp.dot(cond, w_ref[...], preferred_element_type=F32,
                         precision=lax.Precision.HIGHEST) + b_ref[...]


def _adaln(c, w_ada, b_ada):
    depth, d, n = w_ada.shape
    b = c.shape[0]
    tn = 1536
    return pl.pallas_call(
        _adaln_kernel,
        grid=(depth, n // tn),
        in_specs=[
            pl.BlockSpec((b, d), lambda l, j: (0, 0)),
            pl.BlockSpec((None, d, tn), lambda l, j: (l, 0, j)),
            pl.BlockSpec((None, 1, tn), lambda l, j: (l, 0, j)),
        ],
        out_specs=pl.BlockSpec((None, b, tn), lambda l, j: (l, 0, j)),
        out_shape=jax.ShapeDtypeStruct((depth, b, n), F32),
        compiler_params=pltpu.CompilerParams(vmem_limit_bytes=VMEM_LIMIT),
        name="adaln",
    )(c, w_ada, b_ada.reshape(depth, 1, n))


def _mixer_kernel(x_ref, mod_ref, n1g_ref, win_ref, convw_ref, convb_ref, lng_ref, lnb_ref,
                  poolw_ref, poolb_ref, pscale_ref, wout_ref, o_ref,
                  ubuf, zbuf, ya_scr, yb_scr, *, ts, d_conv, conv_width):
    s = pl.program_id(1)
    pool_group = poolw_ref.shape[-1]

    @pl.when(s == 0)
    def _():
        ubuf[0:HALO, :] = jnp.zeros((HALO, ubuf.shape[1]), F32)
        zbuf[0:HALO, :] = jnp.zeros((HALO, zbuf.shape[1]), F32)

    x = x_ref[...]
    h = _rms_modulate(x, n1g_ref[...], mod_ref[0:1, :], mod_ref[1:2, :])
    z = jnp.dot(h.astype(BF16), win_ref[...], preferred_element_type=F32)
    ubuf[HALO:HALO + ts, :] = z[:, :d_conv] * jax.nn.sigmoid(z[:, d_conv:2 * d_conv])
    zbuf[HALO:HALO + ts, :] = z[:, 2 * d_conv:]

    convb = convb_ref[...]
    lng = lng_ref[...]
    lnb = lnb_ref[...]
    for r0 in range(0, ts, CONV_ROWS):
        acc = jnp.broadcast_to(convb, (CONV_ROWS, d_conv))
        for rho in range(SUBLANES):
            part = None
            for q in range((conv_width - 1 - rho) // SUBLANES + 1):
                lag = SUBLANES * q + rho
                base = HALO + r0 - SUBLANES - SUBLANES * q
                term = (convw_ref[conv_width - 1 - lag:conv_width - lag, :]
                        * ubuf[base:base + CONV_ROWS + SUBLANES, :])
                part = term if part is None else part + term
            acc = acc + part[SUBLANES - rho:SUBLANES - rho + CONV_ROWS, :]
        mu = jnp.mean(acc, axis=-1, keepdims=True)
        cen = acc - mu
        var = jnp.mean(cen * cen, axis=-1, keepdims=True)
        yn = cen * lax.rsqrt(var + EPS) * lng + lnb
        ya_scr[r0:r0 + CONV_ROWS, :] = (yn * jax.nn.sigmoid(yn)).astype(BF16)

    frame = (s * ts + lax.broadcasted_iota(jnp.int32, (ts, 1), 0) + 1).astype(F32)
    for g, w in enumerate(POOL_WINDOWS):
        lo = g * pool_group
        zg = zbuf[HALO:HALO + ts, lo:lo + pool_group]
        tot = zg
        for j in range(1, w):
            tot = tot + zbuf[HALO - j:HALO - j + ts, lo:lo + pool_group]
        p = tot / jnp.minimum(frame, float(w)) - zg
        yg = jnp.dot(p.astype(BF16), poolw_ref[g], preferred_element_type=F32) + poolb_ref[g:g + 1, :]
        yb_scr[:, lo:lo + pool_group] = (yg * pscale_ref[:, lo:lo + pool_group]).astype(BF16)

    mix = jnp.dot(ya_scr[...], wout_ref[0:d_conv, :], preferred_element_type=F32)
    mix = mix + jnp.dot(yb_scr[...], wout_ref[d_conv:, :], preferred_element_type=F32)
    o_ref[...] = x + mod_ref[2:3, :] * mix

    ubuf[0:HALO, :] = ubuf[ts:ts + HALO, :]
    zbuf[0:HALO, :] = zbuf[ts:ts + HALO, :]


def _mixer(x, mod, n1g, w_in, conv_w, conv_b, ln_g, ln_b, pool_w, pool_b, pool_scale, w_out, l):
    b, s, d = x.shape
    d_in = w_in.shape[-1]
    conv_width, d_conv = conv_w.shape[1:]
    n_groups, pool_group = pool_w.shape[1:3]
    d_pool = n_groups * pool_group
    ts = min(256, s)
    assert s % ts == 0 and ts % CONV_ROWS == 0 and ts >= HALO
    assert HALO >= SUBLANES * ((conv_width - 1) // SUBLANES + 1) and HALO >= max(POOL_WINDOWS) - 1
    assert d_in == 2 * d_conv + d_pool and n_groups == len(POOL_WINDOWS)
    lay = lambda bi, si: (l, 0, 0)
    kern = functools.partial(_mixer_kernel, ts=ts, d_conv=d_conv, conv_width=conv_width)
    return pl.pallas_call(
        kern,
        grid=(b, s // ts),
        in_specs=[
            pl.BlockSpec((None, ts, d), lambda bi, si: (bi, si, 0)),
            pl.BlockSpec((None, None, 6, d), lambda bi, si: (l, bi, 0, 0)),
            pl.BlockSpec((None, 1, d), lay),
            pl.BlockSpec((None, d, d_in), lay),
            pl.BlockSpec((None, conv_width, d_conv), lay),
            pl.BlockSpec((None, 1, d_conv), lay),
            pl.BlockSpec((None, 1, d_conv), lay),
            pl.BlockSpec((None, 1, d_conv), lay),
            pl.BlockSpec((None, n_groups, pool_group, pool_group), lambda bi, si: (l, 0, 0, 0)),
            pl.BlockSpec((None, n_groups, pool_group), lay),
            pl.BlockSpec((None, 1, d_pool), lay),
            pl.BlockSpec((None, d_conv + d_pool, d), lay),
        ],
        out_specs=pl.BlockSpec((None, ts, d), lambda bi, si: (bi, si, 0)),
        out_shape=jax.ShapeDtypeStruct((b, s, d), F32),
        scratch_shapes=[
            pltpu.VMEM((HALO + ts, d_conv), F32),
            pltpu.VMEM((HALO + ts, d_pool), F32),
            pltpu.VMEM((ts, d_conv), BF16),
            pltpu.VMEM((ts, d_pool), BF16),
        ],
        compiler_params=pltpu.CompilerParams(
            dimension_semantics=("arbitrary", "arbitrary"), vmem_limit_bytes=VMEM_LIMIT),
        name="mixer",
    )(x, mod, n1g, w_in, conv_w, conv_b, ln_g, ln_b, pool_w, pool_b, pool_scale, w_out)


def _sort_network(n):
    pairs = []
    p = 1
    while p < n:
        k = p
        while k >= 1:
            for j in range(k % p, n - k, 2 * k):
                for i in range(min(k, n - j - k)):
                    if (i + j) // (2 * p) == (i + j + k) // (2 * p):
                        pairs.append((i + j, i + j + k))
            k //= 2
        p *= 2
    return pairs


def _topk_rows(sc, k, n_keys):
    nv = n_keys // 8
    tt = sc.shape[1]
    assert k <= nv and nv & (nv - 1) == 0
    sub = lax.broadcasted_iota(jnp.int32, (8, tt), 0).astype(F32)
    val = [sc[8 * i:8 * (i + 1), :] for i in range(nv)]
    idx = [sub + float(8 * i) for i in range(nv)]
    for a, b in _sort_network(nv):
        va, vb, ia, ib = val[a], val[b], idx[a], idx[b]
        swap = (vb > va) | ((vb == va) & (ib < ia))
        val[a], val[b] = jnp.maximum(va, vb), jnp.minimum(va, vb)
        idx[a], idx[b] = jnp.where(swap, ib, ia), jnp.where(swap, ia, ib)
    top_v, top_i = [], []
    for j in range(k):
        m = jnp.max(val[0], axis=0, keepdims=True)
        win = jnp.min(jnp.where(val[0] == m, idx[0], float(n_keys)), axis=0, keepdims=True)
        top_v.append(m)
        top_i.append(win)
        popped = idx[0] == win
        for i in range(nv - 1 - j):
            val[i] = jnp.where(popped, val[i + 1], val[i])
            idx[i] = jnp.where(popped, idx[i + 1], idx[i])
    return top_v, top_i


def _route_kernel(x_ref, mod_ref, n2g_ref, wq_ref, keys_ref, h_ref, i1_ref, i2_ref, gate_ref,
                  q_scr, tv_scr, ti_scr, a_scr, b_scr, e_scr, g_scr, *, tt, heads, n_keys):
    k = PEER_TOPK
    x = x_ref[...]
    h = _rms_modulate(x, n2g_ref[...], mod_ref[3:4, :], mod_ref[4:5, :]).astype(BF16)
    h_ref[...] = h
    q = jnp.dot(h, wq_ref[...], preferred_element_type=F32).astype(BF16)
    half = keys_ref.shape[-1]
    for hp in range(2 * heads):
        q_scr[hp] = q[:, hp * half:(hp + 1) * half]

    def stage1(hd, carry):
        for p in range(2):
            hp = 2 * hd + p
            sc = lax.dot_general(keys_ref[hp], q_scr[hp], (((1,), (1,)), ((), ())),
                                 preferred_element_type=F32)
            top_v, top_i = _topk_rows(sc, k, n_keys)
            for j in range(k):
                tv_scr[hp, j:j + 1, :] = top_v[j]
                ti_scr[hp, j:j + 1, :] = top_i[j]
        return carry

    lax.fori_loop(0, heads, stage1, 0)

    hk = k // 2
    row_lo = lax.broadcasted_iota(jnp.int32, (hk, tt), 0).astype(F32)
    row_hi = row_lo + float(hk)

    def stage2(hd, carry):
        base = pl.multiple_of(hd * k, k)
        tv1 = tv_scr[2 * hd]
        tv2 = tv_scr[2 * hd + 1]
        head_lo = tv1[0:hk, :] + tv2[0:1, :]
        head_hi = tv1[hk:, :] + tv2[0:1, :]
        rest = [tv1[0:hk, :] + tv2[b:b + 1, :] for b in range(1, k)]
        ptr = jnp.zeros((hk, tt), F32)
        best = []
        for j in range(k):
            m = jnp.maximum(jnp.max(head_lo, axis=0, keepdims=True),
                            jnp.max(head_hi, axis=0, keepdims=True))
            a = jnp.minimum(
                jnp.min(jnp.where(head_lo == m, row_lo, float(k)), axis=0, keepdims=True),
                jnp.min(jnp.where(head_hi == m, row_hi, float(k)), axis=0, keepdims=True))
            pop_lo = row_lo == a
            a_scr[pl.ds(base + j, 1), :] = a
            b_scr[pl.ds(base + j, 1), :] = jnp.max(jnp.where(pop_lo, ptr, 0.0), axis=0, keepdims=True)
            best.append(m)
            ptr = jnp.where(pop_lo, ptr + 1.0, ptr)
            head_hi = jnp.where(row_hi == a, NEG_INF, head_hi)
            if j < k - 1:
                head_lo = jnp.where(pop_lo, rest[0], head_lo)
            for i in range(k - 2 - j):
                rest[i] = jnp.where(pop_lo, rest[i + 1], rest[i])
        ex = [jnp.exp(m - best[0]) for m in best]
        den = ex[0]
        for j in range(1, k):
            den = den + ex[j]
        for j in range(k):
            g_scr[pl.ds(base + j, 1), :] = ex[j] / den
        sel_a = a_scr[pl.ds(base, k), :]
        sel_b = b_scr[pl.ds(base, k), :]
        key1 = jnp.zeros((k, tt), F32)
        key2 = jnp.zeros((k, tt), F32)
        for r in range(k):
            key1 = jnp.where(sel_a == float(r), ti_scr[2 * hd, r:r + 1, :], key1)
            key2 = jnp.where(sel_b == float(r), ti_scr[2 * hd + 1, r:r + 1, :], key2)
        e_scr[pl.ds(base, k), :] = key1 * float(n_keys) + key2
        return carry

    lax.fori_loop(0, heads, stage2, 0)

    e = e_scr[...].T.astype(jnp.int32)
    key_bits = n_keys.bit_length() - 1
    i1_ref[...] = lax.shift_right_logical(e, key_bits)
    i2_ref[...] = lax.bitwise_and(e, n_keys - 1)
    gate_ref[...] = g_scr[...].T


def _route(x2, mod, n2g, wq, keys, l, seq):
    t, d = x2.shape
    heads, _, n_keys, half = keys.shape[1:]
    m = heads * PEER_TOPK
    tt = 1024
    assert seq % tt == 0 and m == LANES and n_keys == LANES and n_keys & (n_keys - 1) == 0
    keys2 = keys.reshape(keys.shape[0], 2 * heads, n_keys, half)
    kern = functools.partial(_route_kernel, tt=tt, heads=heads, n_keys=n_keys)
    tok = lambda i: (i, 0)
    return pl.pallas_call(
        kern,
        grid=(t // tt,),
        in_specs=[
            pl.BlockSpec((tt, d), tok),
            pl.BlockSpec((None, None, 6, d), lambda i: (l, (i * tt) // seq, 0, 0)),
            pl.BlockSpec((None, 1, d), lambda i: (l, 0, 0)),
            pl.BlockSpec((None, d, 2 * heads * half), lambda i: (l, 0, 0)),
            pl.BlockSpec((None, 2 * heads, n_keys, half), lambda i: (l, 0, 0, 0)),
        ],
        out_specs=[pl.BlockSpec((tt, d), tok), pl.BlockSpec((tt, m), tok),
                   pl.BlockSpec((tt, m), tok), pl.BlockSpec((tt, m), tok)],
        out_shape=[jax.ShapeDtypeStruct((t, d), BF16), jax.ShapeDtypeStruct((t, m), jnp.int32),
                   jax.ShapeDtypeStruct((t, m), jnp.int32), jax.ShapeDtypeStruct((t, m), F32)],
        scratch_shapes=[
            pltpu.VMEM((2 * heads, tt, half), BF16),
            pltpu.VMEM((2 * heads, PEER_TOPK, tt), F32),
            pltpu.VMEM((2 * heads, PEER_TOPK, tt), F32),
            pltpu.VMEM((m, tt), F32),
            pltpu.VMEM((m, tt), F32),
            pltpu.VMEM((m, tt), F32),
            pltpu.VMEM((m, tt), F32),
        ],
        compiler_params=pltpu.CompilerParams(
            dimension_semantics=("arbitrary",), vmem_limit_bytes=VMEM_LIMIT),
        name="route",
    )(x2, mod, n2g, wq, keys2)


def _expert_in_kernel(h_ref, u_ref, i1_ref, i2_ref, a_ref, *, te):
    j = pl.program_id(1)

    @pl.when(j == 0)
    def _():
        a_ref[...] = jnp.zeros(a_ref.shape, F32)

    h = h_ref[...]
    i1 = i1_ref[...]
    i2 = i2_ref[...]
    acc = a_ref[...]
    for cb in range(te // MXU_COLS):
        a_blk = lax.dot_general(h, u_ref[cb * MXU_COLS:(cb + 1) * MXU_COLS, :], (((1,), (1,)), ((), ())),
                                preferred_element_type=F32)
        for cc in range(MXU_COLS // LANES):
            c = cb * (MXU_COLS // LANES) + cc
            picked = jnp.take_along_axis(a_blk[:, cc * LANES:(cc + 1) * LANES], i2, axis=1)
            acc = jnp.where(i1 == j * (te // LANES) + c, picked, acc)
    a_ref[...] = acc


def _expert_in(h2, ub, i1, i2, l):
    t, d = h2.shape
    n_exp = ub.shape[1]
    m = i1.shape[1]
    tm, te = min(1024, t), 2048
    assert t % tm == 0 and n_exp % te == 0
    tok = lambda i, j: (i, 0)
    return pl.pallas_call(
        functools.partial(_expert_in_kernel, te=te),
        grid=(t // tm, n_exp // te),
        in_specs=[
            pl.BlockSpec((tm, d), tok),
            pl.BlockSpec((None, te, d), lambda i, j: (l, j, 0)),
            pl.BlockSpec((tm, m), tok),
            pl.BlockSpec((tm, m), tok),
        ],
        out_specs=pl.BlockSpec((tm, m), tok),
        out_shape=jax.ShapeDtypeStruct((t, m), F32),
        compiler_params=pltpu.CompilerParams(
            dimension_semantics=("arbitrary", "arbitrary"), vmem_limit_bytes=VMEM_LIMIT),
        name="expert_in",
    )(h2, ub, i1, i2)


def _expert_out_kernel(a_ref, gate_ref, i1_ref, i2_ref, v_ref, x_ref, mod_ref, fg_ref, o_ref,
                       w_scr, pbuf, lhs_scr, *, tm, te, n_keys, final_norm):
    j = pl.program_id(1)
    nk = te // LANES

    @pl.when(j == 0)
    def _():
        a = a_ref[...]
        w_scr[...] = gate_ref[...] * (0.5 * a * (1.0 + lax.erf(a * (2.0 ** -0.5))))
        o_ref[...] = jnp.zeros(o_ref.shape, F32)
        key_id = lax.broadcasted_iota(jnp.int32, (n_keys, LANES), 0)

        def place(t, carry):
            w_row = w_scr[pl.ds(t, 1), :]
            left = jnp.where(key_id == i1_ref[pl.ds(t, 1), :], w_row, 0.0).astype(BF16)
            right = jnp.where(key_id == i2_ref[pl.ds(t, 1), :], 1.0, 0.0).astype(BF16)
            pbuf[pl.ds(pl.multiple_of(t * P_PITCH, 8), n_keys), :] = lax.dot_general(
                left, right, (((1,), (1,)), ((), ())), preferred_element_type=F32)
            return carry

        lax.fori_loop(0, tm, place, 0, unroll=32)

    for c in range(nk):
        lhs_scr[:, c * LANES:(c + 1) * LANES] = pbuf[pl.ds(j * nk + c, tm, stride=P_PITCH), :].astype(BF16)
    o_ref[...] += jnp.dot(lhs_scr[...], v_ref[...], preferred_element_type=F32)

    @pl.when(j == pl.num_programs(1) - 1)
    def _():
        y = x_ref[...] + mod_ref[5:6, :] * o_ref[...]
        if final_norm:
            ms = jnp.mean(y * y, axis=-1, keepdims=True)
            y = y * lax.rsqrt(ms + EPS) * fg_ref[...]
        o_ref[...] = y


def _expert_out(a_sel, gate, i1, i2, vb, x2, mod, final_g, l, seq, final_norm):
    t, d = x2.shape
    n_exp = vb.shape[1]
    m = i1.shape[1]
    n_keys = LANES
    tm, te = 512, 2048
    assert t % tm == 0 and seq % tm == 0 and n_exp % te == 0 and n_exp == n_keys * n_keys
    tok = lambda i, j: (i, 0)
    kern = functools.partial(_expert_out_kernel, tm=tm, te=te, n_keys=n_keys, final_norm=final_norm)
    return pl.pallas_call(
        kern,
        grid=(t // tm, n_exp // te),
        in_specs=[
            pl.BlockSpec((tm, m), tok),
            pl.BlockSpec((tm, m), tok),
            pl.BlockSpec((tm, m), tok),
            pl.BlockSpec((tm, m), tok),
            pl.BlockSpec((None, te, d), lambda i, j: (l, j, 0)),
            pl.BlockSpec((tm, d), tok, pipeline_mode=pl.Buffered(1)),
            pl.BlockSpec((None, None, 6, d), lambda i, j: (l, (i * tm) // seq, 0, 0)),
            pl.BlockSpec((1, d), lambda i, j: (0, 0)),
        ],
        out_specs=pl.BlockSpec((tm, d), tok),
        out_shape=jax.ShapeDtypeStruct((t, d), F32),
        scratch_shapes=[
            pltpu.VMEM((tm, m), F32),
            pltpu.VMEM((tm * P_PITCH, LANES), F32),
            pltpu.VMEM((tm, te), BF16),
        ],
        compiler_params=pltpu.CompilerParams(
            dimension_semantics=("arbitrary", "arbitrary"), vmem_limit_bytes=VMEM_LIMIT),
        name="expert_out",
    )(a_sel, gate, i1, i2, vb, x2, mod, final_g)


def kernel(x, c, w_ada, b_ada, norm1_g, norm2_g, w_in, w_out, conv_w, conv_b, conv_ln_g, conv_ln_b,
           pool_w, pool_b, pool_scale, peer_wq, peer_keys, peer_u, peer_v, final_g):
    depth = w_ada.shape[0]
    b, s, d = x.shape
    row = lambda p: p.reshape(depth, 1, p.shape[-1])

    mod = _adaln(c, w_ada, b_ada).reshape(depth, b, 6, d)
    w_in_b = w_in.astype(BF16)
    w_out_b = w_out.astype(BF16)
    pool_w_b = pool_w.astype(BF16)
    wq_b = peer_wq.astype(BF16)
    keys_b = peer_keys.astype(BF16)
    u_b = peer_u.astype(BF16)
    v_b = peer_v.astype(BF16)
    n1g, n2g = row(norm1_g), row(norm2_g)
    cb, lg, lb, ps = row(conv_b), row(conv_ln_g), row(conv_ln_b), row(pool_scale)
    fg = final_g.reshape(1, d)

    for l in range(depth):
        x = _mixer(x, mod, n1g, w_in_b, conv_w, cb, lg, lb, pool_w_b, pool_b, ps, w_out_b, l)
        x2 = x.reshape(b * s, d)
        h2, i1, i2, gate = _route(x2, mod, n2g, wq_b, keys_b, l, s)
        a_sel = _expert_in(h2, u_b, i1, i2, l)
        x2 = _expert_out(a_sel, gate, i1, i2, v_b, x2, mod, fg, l, s, l == depth - 1)
        x = x2.reshape(b, s, d)
    return x
```

```python
import functools

import jax
import jax.numpy as jnp
from jax import lax
from jax.experimental import pallas as pl
from jax.experimental.pallas import tpu as pltpu

EPS = 1e-6
POOL_WINDOWS = (2, 4, 8, 16)
PEER_TOPK = 16
LANES = 128
SUBLANES = 8
MXU_COLS = 256
HALO = 32
CONV_ROWS = 32
P_PITCH = 136
VMEM_LIMIT = 58 * 1024 * 1024

F32 = jnp.float32
BF16 = jnp.bfloat16
NEG_INF = float("-inf")


def _rms_modulate(x, gain, shift, scale):
    ms = jnp.mean(x * x, axis=-1, keepdims=True)
    return (x * lax.rsqrt(ms + EPS) * gain) * (1.0 + scale) + shift


def _adaln_kernel(c_ref, w_ref, b_ref, o_ref):
    c = c_ref[...]
    cond = c * jax.nn.sigmoid(c)
    o_ref[...] = jnp.dot(cond, w_ref[...], preferred_element_type=F32,
                         precision=lax.Precision.HIGHEST) + b_ref[...]


def _adaln(c, w_ada, b_ada):
    depth, d, n = w_ada.shape
    b = c.shape[0]
    tn = 1536
    return pl.pallas_call(
        _adaln_kernel,
        grid=(depth, n // tn),
        in_specs=[
            pl.BlockSpec((b, d), lambda l, j: (0, 0)),
            pl.BlockSpec((None, d, tn), lambda l, j: (l, 0, j)),
            pl.BlockSpec((None, 1, tn), lambda l, j: (l, 0, j)),
        ],
        out_specs=pl.BlockSpec((None, b, tn), lambda l, j: (l, 0, j)),
        out_shape=jax.ShapeDtypeStruct((depth, b, n), F32),
        compiler_params=pltpu.CompilerParams(vmem_limit_bytes=VMEM_LIMIT),
        name="adaln",
    )(c, w_ada, b_ada.reshape(depth, 1, n))


def _mixer_kernel(x_ref, mod_ref, n1g_ref, win_ref, convw_ref, convb_ref, lng_ref, lnb_ref,
                  poolw_ref, poolb_ref, pscale_ref, wout_ref, o_ref,
                  ubuf, zbuf, ya_scr, yb_scr, *, ts, d_conv, conv_width):
    s = pl.program_id(1)
    pool_group = poolw_ref.shape[-1]

    @pl.when(s == 0)
    def _():
        ubuf[0:HALO, :] = jnp.zeros((HALO, ubuf.shape[1]), F32)
        zbuf[0:HALO, :] = jnp.zeros((HALO, zbuf.shape[1]), F32)

    x = x_ref[...]
    h = _rms_modulate(x, n1g_ref[...], mod_ref[0:1, :], mod_ref[1:2, :])
    z = jnp.dot(h.astype(BF16), win_ref[...], preferred_element_type=F32)
    ubuf[HALO:HALO + ts, :] = z[:, :d_conv] * jax.nn.sigmoid(z[:, d_conv:2 * d_conv])
    zbuf[HALO:HALO + ts, :] = z[:, 2 * d_conv:]

    convb = convb_ref[...]
    lng = lng_ref[...]
    lnb = lnb_ref[...]
    for r0 in range(0, ts, CONV_ROWS):
        acc = jnp.broadcast_to(convb, (CONV_ROWS, d_conv))
        for rho in range(SUBLANES):
            part = None
            for q in range((conv_width - 1 - rho) // SUBLANES + 1):
                lag = SUBLANES * q + rho
                base = HALO + r0 - SUBLANES - SUBLANES * q
                term = (convw_ref[conv_width - 1 - lag:conv_width - lag, :]
                        * ubuf[base:base + CONV_ROWS + SUBLANES, :])
                part = term if part is None else part + term
            acc = acc + part[SUBLANES - rho:SUBLANES - rho + CONV_ROWS, :]
        mu = jnp.mean(acc, axis=-1, keepdims=True)
        cen = acc - mu
        var = jnp.mean(cen * cen, axis=-1, keepdims=True)
        yn = cen * lax.rsqrt(var + EPS) * lng + lnb
        ya_scr[r0:r0 + CONV_ROWS, :] = (yn * jax.nn.sigmoid(yn)).astype(BF16)

    frame = (s * ts + lax.broadcasted_iota(jnp.int32, (ts, 1), 0) + 1).astype(F32)
    for g, w in enumerate(POOL_WINDOWS):
        lo = g * pool_group
        zg = zbuf[HALO:HALO + ts, lo:lo + pool_group]
        tot = zg
        for j in range(1, w):
            tot = tot + zbuf[HALO - j:HALO - j + ts, lo:lo + pool_group]
        p = tot / jnp.minimum(frame, float(w)) - zg
        yg = jnp.dot(p.astype(BF16), poolw_ref[g], preferred_element_type=F32) + poolb_ref[g:g + 1, :]
        yb_scr[:, lo:lo + pool_group] = (yg * pscale_ref[:, lo:lo + pool_group]).astype(BF16)

    mix = jnp.dot(ya_scr[...], wout_ref[0:d_conv, :], preferred_element_type=F32)
    mix = mix + jnp.dot(yb_scr[...], wout_ref[d_conv:, :], preferred_element_type=F32)
    o_ref[...] = x + mod_ref[2:3, :] * mix

    ubuf[0:HALO, :] = ubuf[ts:ts + HALO, :]
    zbuf[0:HALO, :] = zbuf[ts:ts + HALO, :]


def _mixer(x, mod, n1g, w_in, conv_w, conv_b, ln_g, ln_b, pool_w, pool_b, pool_scale, w_out, l):
    b, s, d = x.shape
    d_in = w_in.shape[-1]
    conv_width, d_conv = conv_w.shape[1:]
    n_groups, pool_group = pool_w.shape[1:3]
    d_pool = n_groups * pool_group
    ts = min(512, s)
    assert s % ts == 0 and ts % CONV_ROWS == 0 and ts >= HALO
    assert HALO >= SUBLANES * ((conv_width - 1) // SUBLANES + 1) and HALO >= max(POOL_WINDOWS) - 1
    assert d_in == 2 * d_conv + d_pool and n_groups == len(POOL_WINDOWS)
    lay = lambda bi, si: (l, 0, 0)
    kern = functools.partial(_mixer_kernel, ts=ts, d_conv=d_conv, conv_width=conv_width)
    return pl.pallas_call(
        kern,
        grid=(b, s // ts),
        in_specs=[
            pl.BlockSpec((None, ts, d), lambda bi, si: (bi, si, 0)),
            pl.BlockSpec((None, None, 6, d), lambda bi, si: (l, bi, 0, 0)),
            pl.BlockSpec((None, 1, d), lay),
            pl.BlockSpec((None, d, d_in), lay),
            pl.BlockSpec((None, conv_width, d_conv), lay),
            pl.BlockSpec((None, 1, d_conv), lay),
            pl.BlockSpec((None, 1, d_conv), lay),
            pl.BlockSpec((None, 1, d_conv), lay),
            pl.BlockSpec((None, n_groups, pool_group, pool_group), lambda bi, si: (l, 0, 0, 0)),
            pl.BlockSpec((None, n_groups, pool_group), lay),
            pl.BlockSpec((None, 1, d_pool), lay),
            pl.BlockSpec((None, d_conv + d_pool, d), lay),
        ],
        out_specs=pl.BlockSpec((None, ts, d), lambda bi, si: (bi, si, 0)),
        out_shape=jax.ShapeDtypeStruct((b, s, d), F32),
        scratch_shapes=[
            pltpu.VMEM((HALO + ts, d_conv), F32),
            pltpu.VMEM((HALO + ts, d_pool), F32),
            pltpu.VMEM((ts, d_conv), BF16),
            pltpu.VMEM((ts, d_pool), BF16),
        ],
        compiler_params=pltpu.CompilerParams(
            dimension_semantics=("arbitrary", "arbitrary"), vmem_limit_bytes=VMEM_LIMIT),
        name="mixer",
    )(x, mod, n1g, w_in, conv_w, conv_b, ln_g, ln_b, pool_w, pool_b, pool_scale, w_out)


def _sort_network(n):
    pairs = []
    p = 1
    while p < n:
        k = p
        while k >= 1:
            for j in range(k % p, n - k, 2 * k):
                for i in range(min(k, n - j - k)):
                    if (i + j) // (2 * p) == (i + j + k) // (2 * p):
                        pairs.append((i + j, i + j + k))
            k //= 2
        p *= 2
    return pairs


def _topk_rows(sc, k, n_keys):
    nv = n_keys // 8
    tt = sc.shape[1]
    assert k <= nv and nv & (nv - 1) == 0
    sub = lax.broadcasted_iota(jnp.int32, (8, tt), 0).astype(F32)
    val = [sc[8 * i:8 * (i + 1), :] for i in range(nv)]
    idx = [sub + float(8 * i) for i in range(nv)]
    for a, b in _sort_network(nv):
        va, vb, ia, ib = val[a], val[b], idx[a], idx[b]
        swap = (vb > va) | ((vb == va) & (ib < ia))
        val[a], val[b] = jnp.maximum(va, vb), jnp.minimum(va, vb)
        idx[a], idx[b] = jnp.where(swap, ib, ia), jnp.where(swap, ia, ib)
    top_v, top_i = [], []
    for j in range(k):
        m = jnp.max(val[0], axis=0, keepdims=True)
        win = jnp.min(jnp.where(val[0] == m, idx[0], float(n_keys)), axis=0, keepdims=True)
        top_v.append(m)
        top_i.append(win)
        popped = idx[0] == win
        for i in range(nv - 1 - j):
            val[i] = jnp.where(popped, val[i + 1], val[i])
            idx[i] = jnp.where(popped, idx[i + 1], idx[i])
    return top_v, top_i


def _route_kernel(x_ref, mod_ref, n2g_ref, wq_ref, keys_ref, h_ref, i1_ref, i2_ref, gate_ref,
                  q_scr, tv_scr, ti_scr, a_scr, b_scr, e_scr, g_scr, *, tt, heads, n_keys):
    k = PEER_TOPK
    x = x_ref[...]
    h = _rms_modulate(x, n2g_ref[...], mod_ref[3:4, :], mod_ref[4:5, :]).astype(BF16)
    h_ref[...] = h
    q = jnp.dot(h, wq_ref[...], preferred_element_type=F32).astype(BF16)
    half = keys_ref.shape[-1]
    for hp in range(2 * heads):
        q_scr[hp] = q[:, hp * half:(hp + 1) * half]

    def stage1(hd, carry):
        for p in range(2):
            hp = 2 * hd + p
            sc = lax.dot_general(keys_ref[hp], q_scr[hp], (((1,), (1,)), ((), ())),
                                 preferred_element_type=F32)
            top_v, top_i = _topk_rows(sc, k, n_keys)
            for j in range(k):
                tv_scr[hp, j:j + 1, :] = top_v[j]
                ti_scr[hp, j:j + 1, :] = top_i[j]
        return carry

    lax.fori_loop(0, heads, stage1, 0)

    hk = k // 2
    row_lo = lax.broadcasted_iota(jnp.int32, (hk, tt), 0).astype(F32)
    row_hi = row_lo + float(hk)

    def stage2(hd, carry):
        base = pl.multiple_of(hd * k, k)
        tv1 = tv_scr[2 * hd]
        tv2 = tv_scr[2 * hd + 1]
        head_lo = tv1[0:hk, :] + tv2[0:1, :]
        head_hi = tv1[hk:, :] + tv2[0:1, :]
        rest = [tv1[0:hk, :] + tv2[b:b + 1, :] for b in range(1, k)]
        ptr = jnp.zeros((hk, tt), F32)
        best = []
        for j in range(k):
            m = jnp.maximum(jnp.max(head_lo, axis=0, keepdims=True),
                            jnp.max(head_hi, axis=0, keepdims=True))
            a = jnp.minimum(
                jnp.min(jnp.where(head_lo == m, row_lo, float(k)), axis=0, keepdims=True),
                jnp.min(jnp.where(head_hi == m, row_hi, float(k)), axis=0, keepdims=True))
            pop_lo = row_lo == a
            a_scr[pl.ds(base + j, 1), :] = a
            b_scr[pl.ds(base + j, 1), :] = jnp.max(jnp.where(pop_lo, ptr, 0.0), axis=0, keepdims=True)
            best.append(m)
            ptr = jnp.where(pop_lo, ptr + 1.0, ptr)
            head_hi = jnp.where(row_hi == a, NEG_INF, head_hi)
            if j < k - 1:
                head_lo = jnp.where(pop_lo, rest[0], head_lo)
            for i in range(k - 2 - j):
                rest[i] = jnp.where(pop_lo, rest[i + 1], rest[i])
        ex = [jnp.exp(m - best[0]) for m in best]
        den = ex[0]
        for j in range(1, k):
            den = den + ex[j]
        for j in range(k):
            g_scr[pl.ds(base + j, 1), :] = ex[j] / den
        sel_a = a_scr[pl.ds(base, k), :]
        sel_b = b_scr[pl.ds(base, k), :]
        key1 = jnp.zeros((k, tt), F32)
        key2 = jnp.zeros((k, tt), F32)
        for r in range(k):
            key1 = jnp.where(sel_a == float(r), ti_scr[2 * hd, r:r + 1, :], key1)
            key2 = jnp.where(sel_b == float(r), ti_scr[2 * hd + 1, r:r + 1, :], key2)
        e_scr[pl.ds(base, k), :] = key1 * float(n_keys) + key2
        return carry

    lax.fori_loop(0, heads, stage2, 0)

    e = e_scr[...].T.astype(jnp.int32)
    key_bits = n_keys.bit_length() - 1
    i1_ref[...] = lax.shift_right_logical(e, key_bits)
    i2_ref[...] = lax.bitwise_and(e, n_keys - 1)
    gate_ref[...] = g_scr[...].T


def _route(x2, mod, n2g, wq, keys, l, seq):
    t, d = x2.shape
    heads, _, n_keys, half = keys.shape[1:]
    m = heads * PEER_TOPK
    tt = 1024
    assert seq % tt == 0 and m == LANES and n_keys == LANES and n_keys & (n_keys - 1) == 0
    keys2 = keys.reshape(keys.shape[0], 2 * heads, n_keys, half)
    kern = functools.partial(_route_kernel, tt=tt, heads=heads, n_keys=n_keys)
    tok = lambda i: (i, 0)
    return pl.pallas_call(
        kern,
        grid=(t // tt,),
        in_specs=[
            pl.BlockSpec((tt, d), tok),
            pl.BlockSpec((None, None, 6, d), lambda i: (l, (i * tt) // seq, 0, 0)),
            pl.BlockSpec((None, 1, d), lambda i: (l, 0, 0)),
            pl.BlockSpec((None, d, 2 * heads * half), lambda i: (l, 0, 0)),
            pl.BlockSpec((None, 2 * heads, n_keys, half), lambda i: (l, 0, 0, 0)),
        ],
        out_specs=[pl.BlockSpec((tt, d), tok), pl.BlockSpec((tt, m), tok),
                   pl.BlockSpec((tt, m), tok), pl.BlockSpec((tt, m), tok)],
        out_shape=[jax.ShapeDtypeStruct((t, d), BF16), jax.ShapeDtypeStruct((t, m), jnp.int32),
                   jax.ShapeDtypeStruct((t, m), jnp.int32), jax.ShapeDtypeStruct((t, m), F32)],
        scratch_shapes=[
            pltpu.VMEM((2 * heads, tt, half), BF16),
            pltpu.VMEM((2 * heads, PEER_TOPK, tt), F32),
            pltpu.VMEM((2 * heads, PEER_TOPK, tt), F32),
            pltpu.VMEM((m, tt), F32),
            pltpu.VMEM((m, tt), F32),
            pltpu.VMEM((m, tt), F32),
            pltpu.VMEM((m, tt), F32),
        ],
        compiler_params=pltpu.CompilerParams(
            dimension_semantics=("arbitrary",), vmem_limit_bytes=VMEM_LIMIT),
        name="route",
    )(x2, mod, n2g, wq, keys2)


def _expert_in_kernel(h_ref, u_ref, i1_ref, i2_ref, a_ref, *, te):
    j = pl.program_id(1)

    @pl.when(j == 0)
    def _():
        a_ref[...] = jnp.zeros(a_ref.shape, F32)

    h = h_ref[...]
    i1 = i1_ref[...]
    i2 = i2_ref[...]
    acc = a_ref[...]
    for cb in range(te // MXU_COLS):
        u_blk = u_ref[cb * MXU_COLS:(cb + 1) * MXU_COLS, :].astype(BF16)
        a_blk = lax.dot_general(h, u_blk, (((1,), (1,)), ((), ())),
                                preferred_element_type=F32)
        for cc in range(MXU_COLS // LANES):
            c = cb * (MXU_COLS // LANES) + cc
            picked = jnp.take_along_axis(a_blk[:, cc * LANES:(cc + 1) * LANES], i2, axis=1)
            acc = jnp.where(i1 == j * (te // LANES) + c, picked, acc)
    a_ref[...] = acc


def _expert_in(h2, ub, i1, i2, l):
    t, d = h2.shape
    n_exp = ub.shape[1]
    m = i1.shape[1]
    tm, te = min(2048, t), 2048
    assert t % tm == 0 and n_exp % te == 0
    tok = lambda i, j: (i, 0)
    return pl.pallas_call(
        functools.partial(_expert_in_kernel, te=te),
        grid=(t // tm, n_exp // te),
        in_specs=[
            pl.BlockSpec((tm, d), tok),
            pl.BlockSpec((None, te, d), lambda i, j: (l, j, 0)),
            pl.BlockSpec((tm, m), tok),
            pl.BlockSpec((tm, m), tok),
        ],
        out_specs=pl.BlockSpec((tm, m), tok),
        out_shape=jax.ShapeDtypeStruct((t, m), F32),
        compiler_params=pltpu.CompilerParams(
            dimension_semantics=("arbitrary", "arbitrary"), vmem_limit_bytes=VMEM_LIMIT),
        name="expert_in",
    )(h2, ub, i1, i2)


def _expert_out_kernel(a_ref, gate_ref, i1_ref, i2_ref, v_ref, x_ref, mod_ref, fg_ref, o_ref,
                       w_scr, pbuf, lhs_scr, *, tm, te, n_keys, final_norm):
    j = pl.program_id(1)
    nk = te // LANES

    @pl.when(j == 0)
    def _():
        a = a_ref[...]
        w_scr[...] = gate_ref[...] * (0.5 * a * (1.0 + lax.erf(a * (2.0 ** -0.5))))
        o_ref[...] = jnp.zeros(o_ref.shape, F32)
        key_id = lax.broadcasted_iota(jnp.int32, (n_keys, LANES), 0)

        def place(t, carry):
            w_row = w_scr[pl.ds(t, 1), :]
            left = jnp.where(key_id == i1_ref[pl.ds(t, 1), :], w_row, 0.0).astype(BF16)
            right = jnp.where(key_id == i2_ref[pl.ds(t, 1), :], 1.0, 0.0).astype(BF16)
            pbuf[pl.ds(pl.multiple_of(t * P_PITCH, 8), n_keys), :] = lax.dot_general(
                left, right, (((1,), (1,)), ((), ())), preferred_element_type=F32)
            return carry

        lax.fori_loop(0, tm, place, 0, unroll=32)

    for c in range(nk):
        lhs_scr[:, c * LANES:(c + 1) * LANES] = pbuf[pl.ds(j * nk + c, tm, stride=P_PITCH), :].astype(BF16)
    o_ref[...] += jnp.dot(lhs_scr[...], v_ref[...], preferred_element_type=F32)

    @pl.when(j == pl.num_programs(1) - 1)
    def _():
        y = x_ref[...] + mod_ref[5:6, :] * o_ref[...]
        if final_norm:
            ms = jnp.mean(y * y, axis=-1, keepdims=True)
            y = y * lax.rsqrt(ms + EPS) * fg_ref[...]
        o_ref[...] = y


def _expert_out(a_sel, gate, i1, i2, vb, x2, mod, final_g, l, seq, final_norm):
    t, d = x2.shape
    n_exp = vb.shape[1]
    m = i1.shape[1]
    n_keys = LANES
    tm, te = 512, 2048
    assert t % tm == 0 and seq % tm == 0 and n_exp % te == 0 and n_exp == n_keys * n_keys
    tok = lambda i, j: (i, 0)
    kern = functools.partial(_expert_out_kernel, tm=tm, te=te, n_keys=n_keys, final_norm=final_norm)
    return pl.pallas_call(
        kern,
        grid=(t // tm, n_exp // te),
        in_specs=[
            pl.BlockSpec((tm, m), tok),
            pl.BlockSpec((tm, m), tok),
            pl.BlockSpec((tm, m), tok),
            pl.BlockSpec((tm, m), tok),
            pl.BlockSpec((None, te, d), lambda i, j: (l, j, 0)),
            pl.BlockSpec((tm, d), tok, pipeline_mode=pl.Buffered(1)),
            pl.BlockSpec((None, None, 6, d), lambda i, j: (l, (i * tm) // seq, 0, 0)),
            pl.BlockSpec((1, d), lambda i, j: (0, 0)),
        ],
        out_specs=pl.BlockSpec((tm, d), tok),
        out_shape=jax.ShapeDtypeStruct((t, d), F32),
        scratch_shapes=[
            pltpu.VMEM((tm, m), F32),
            pltpu.VMEM((tm * P_PITCH, LANES), F32),
            pltpu.VMEM((tm, te), BF16),
        ],
        compiler_params=pltpu.CompilerParams(
            dimension_semantics=("arbitrary", "arbitrary"), vmem_limit_bytes=VMEM_LIMIT),
        name="expert_out",
    )(a_sel, gate, i1, i2, vb, x2, mod, final_g)


def kernel(x, c, w_ada, b_ada, norm1_g, norm2_g, w_in, w_out, conv_w, conv_b, conv_ln_g, conv_ln_b,
           pool_w, pool_b, pool_scale, peer_wq, peer_keys, peer_u, peer_v, final_g):
    depth = w_ada.shape[0]
    b, s, d = x.shape
    row = lambda p: p.reshape(depth, 1, p.shape[-1])

    mod = _adaln(c, w_ada, b_ada).reshape(depth, b, 6, d)
    w_in_b = w_in.astype(BF16)
    w_out_b = w_out.astype(BF16)
    pool_w_b = pool_w.astype(BF16)
    wq_b = peer_wq.astype(BF16)
    keys_b = peer_keys.astype(BF16)
    v_b = peer_v.astype(BF16)
    n1g, n2g = row(norm1_g), row(norm2_g)
    cb, lg, lb, ps = row(conv_b), row(conv_ln_g), row(conv_ln_b), row(pool_scale)
    fg = final_g.reshape(1, d)

    for l in range(depth):
        x = _mixer(x, mod, n1g, w_in_b, conv_w, cb, lg, lb, pool_w_b, pool_b, ps, w_out_b, l)
        x2 = x.reshape(b * s, d)
        h2, i1, i2, gate = _route(x2, mod, n2g, wq_b, keys_b, l, s)
        a_sel = _expert_in(h2, peer_u, i1, i2, l)
        x2 = _expert_out(a_sel, gate, i1, i2, v_b, x2, mod, fg, l, s, l == depth - 1)
        x = x2.reshape(b, s, d)
    return x
```
